```python
import jax
import jax.numpy as jnp
from jax import lax
import numpy as np

D_MODEL = 1024
BATCH = 8
SEQ = 4096
DEPTH = 1
DEC_BATCH = 128
DEC_SEQ = 1
PAST_LEN = 8192
PAGE_SIZE = 128

D_RNN = 512
RNN_BLOCKS = 8
RNN_BLOCK_W = D_RNN // RNN_BLOCKS
CONV_W = 4
LRU_C = 8.0
N_HEADS = 8
N_KV_HEADS = 8
HEAD_DIM = 64
IDX_HEADS = 8
IDX_DIM = 64
TOPK_MAX = 256
Q_BLOCK = 128
D_FF = -(-8 * D_MODEL // (3 * 256)) * 256
ROPE_THETA = 10000.0
EPS = 1e-6
IN_SIZES = (D_RNN, D_RNN, N_HEADS * HEAD_DIM, N_KV_HEADS * HEAD_DIM, N_KV_HEADS * HEAD_DIM,
            IDX_HEADS * IDX_DIM, IDX_DIM, IDX_HEADS, D_MODEL, D_MODEL)
D_IN = 2 * D_RNN + (N_HEADS + 2 * N_KV_HEADS) * HEAD_DIM + IDX_HEADS * IDX_DIM + IDX_DIM + IDX_HEADS + 2 * D_MODEL

kernel_name = "hybrid_rglru_dsa_decoder_step"


def rmsnorm(x, g):
    xf = x.astype(jnp.float32)
    y = xf * lax.rsqrt(jnp.mean(xf * xf, axis=-1, keepdims=True) + EPS)
    return (y * g.astype(jnp.float32)).astype(x.dtype)


def rope(x, pos):
    half = x.shape[-1] // 2
    inv = 1.0 / (ROPE_THETA ** (jnp.arange(half, dtype=jnp.float32) / half))
    ang = pos.astype(jnp.float32)[:, None] * inv[None, :]
    cos = jnp.cos(ang)[None, :, None, :]
    sin = jnp.sin(ang)[None, :, None, :]
    xf = x.astype(jnp.float32)
    x1, x2 = xf[..., :half], xf[..., half:]
    return jnp.concatenate([x1 * cos - x2 * sin, x2 * cos + x1 * sin], axis=-1).astype(x.dtype)


def project(x, pos, norm1_g, w_in, q_norm_g, k_norm_g, k_idx_norm_g):
    B, T, _ = x.shape
    z = rmsnorm(x, norm1_g) @ w_in
    splits = np.cumsum(IN_SIZES)[:-1].tolist()
    xr, gr, q, k, v, qi, ki, wi, ga, gb = jnp.split(z, splits, axis=-1)
    q = rope(rmsnorm(q.reshape(B, T, N_HEADS, HEAD_DIM), q_norm_g), pos)
    k = rope(rmsnorm(k.reshape(B, T, N_KV_HEADS, HEAD_DIM), k_norm_g), pos)
    v = v.reshape(B, T, N_KV_HEADS, HEAD_DIM)
    qi = rope(qi.reshape(B, T, IDX_HEADS, IDX_DIM), pos)
    ki = rope(rmsnorm(ki, k_idx_norm_g)[:, :, None, :], pos)[:, :, 0, :]
    return xr, gr, q, k, v, qi, ki, wi, ga, gb


def _lin_combine(c1, c2):
    a1, b1 = c1
    a2, b2 = c2
    return a1 * a2, a2 * b1 + b2


def rglru_branch(xr, gr, conv_buf, h0, conv_w, conv_b, w_rg, b_rg, w_ig, b_ig, lru_lambda):
    B, T, _ = xr.shape
    xpad = jnp.concatenate([conv_buf.astype(xr.dtype), xr], axis=1)
    xc = conv_b + conv_w[0] * xpad[:, 0:T]
    for j in range(1, CONV_W):
        xc = xc + conv_w[j] * xpad[:, j:j + T]
    xb = xc.reshape(B, T, RNN_BLOCKS, RNN_BLOCK_W)
    r = jax.nn.sigmoid((jnp.einsum('btnd,nde->btne', xb, w_rg).reshape(B, T, D_RNN) + b_rg).astype(jnp.float32))
    i = jax.nn.sigmoid((jnp.einsum('btnd,nde->btne', xb, w_ig).reshape(B, T, D_RNN) + b_ig).astype(jnp.float32))
    log_a = -LRU_C * r * jax.nn.softplus(-lru_lambda.astype(jnp.float32))
    a = jnp.exp(log_a)
    b = jnp.sqrt(-jnp.expm1(2.0 * log_a)) * i * xc.astype(jnp.float32)
    b = b.at[:, 0].add(a[:, 0] * h0.astype(jnp.float32))
    _, h = lax.associative_scan(_lin_combine, (a, b), axis=1)
    out = h.astype(xr.dtype) * jax.nn.gelu(gr)
    return out, xpad[:, -(CONV_W - 1):], h[:, -1]


def indexer_topk(qi, wi, ki, qpos, n_sel):
    s = jnp.einsum('bthd,bsd->bths', qi, ki).astype(jnp.float32)
    scores = jnp.einsum('bths,bth->bts', jax.nn.relu(s), wi.astype(jnp.float32)) * (IDX_HEADS ** -0.5 * IDX_DIM ** -0.5)
    kpos = jnp.arange(ki.shape[1])
    scores = jnp.where(kpos[None, None, :] <= qpos[None, :, None], scores, -jnp.inf)
    _, idx = lax.top_k(scores, n_sel)
    valid = idx <= qpos[None, :, None]
    return idx, valid


def sparse_attend(q, k_sel, v_sel, valid):
    B, T, H, D = q.shape
    qg = q.reshape(B, T, N_KV_HEADS, H // N_KV_HEADS, D)
    s = jnp.einsum('btngd,btknd->btngk', qg, k_sel).astype(jnp.float32) * (D ** -0.5)
    s = jnp.where(valid[:, :, None, None, :], s, -jnp.inf)
    p = jax.nn.softmax(s, axis=-1).astype(v_sel.dtype)
    return jnp.einsum('btngk,btknd->btngd', p, v_sel).reshape(B, T, H * D)


def prompt_sparse_attention(q, k, v, qi, ki, wi, n_sel):
    B, S = q.shape[0], q.shape[1]

    def block(bi):
        start = bi * Q_BLOCK
        qb = lax.dynamic_slice_in_dim(q, start, Q_BLOCK, axis=1)
        qib = lax.dynamic_slice_in_dim(qi, start, Q_BLOCK, axis=1)
        wib = lax.dynamic_slice_in_dim(wi, start, Q_BLOCK, axis=1)
        qpos = start + jnp.arange(Q_BLOCK)
        idx, valid = indexer_topk(qib, wib, ki, qpos, n_sel)
        k_sel = jax.vmap(lambda kk, ii: kk[ii])(k, idx)
        v_sel = jax.vmap(lambda vv, ii: vv[ii])(v, idx)
        return sparse_attend(qb, k_sel, v_sel, valid)

    out = lax.map(block, jnp.arange(S // Q_BLOCK))
    return out.transpose(1, 0, 2, 3).reshape(B, S, N_HEADS * HEAD_DIM)


def sample_sparse_attention(q, k_new, v_new, qi, ki_new, wi, cache_k, cache_v, cache_k_idx, layer, page_table, n_sel):
    Bd, T = q.shape[0], q.shape[1]
    past = page_table.shape[1] * PAGE_SIZE
    ki_past = cache_k_idx[layer, page_table].reshape(Bd, past, IDX_DIM)
    ki_all = jnp.concatenate([ki_past.astype(ki_new.dtype), ki_new], axis=1)
    qpos = past + jnp.arange(T)
    idx, valid = indexer_topk(qi, wi, ki_all, qpos, n_sel)
    from_new = (idx >= past)[..., None, None]
    pidx = jnp.minimum(idx, past - 1)
    phys = jax.vmap(lambda pt, pi: pt[pi // PAGE_SIZE])(page_table, pidx)
    row = pidx % PAGE_SIZE
    nidx = jnp.clip(idx - past, 0, T - 1)
    k_sel = jnp.where(from_new, jax.vmap(lambda kk, ii: kk[ii])(k_new, nidx),
                      cache_k[layer, phys, row].astype(k_new.dtype))
    v_sel = jnp.where(from_new, jax.vmap(lambda vv, ii: vv[ii])(v_new, nidx),
                      cache_v[layer, phys, row].astype(v_new.dtype))
    return sparse_attend(q, k_sel, v_sel, valid)


def merge_ffn(x, o_a, o_b, ga, gb, w_branch_a, w_branch_b, w_out, norm2_g, w_ffn_gate, w_ffn_up, w_ffn_down):
    m = jax.nn.sigmoid(ga) * (o_a @ w_branch_a) + jax.nn.sigmoid(gb) * (o_b @ w_branch_b)
    x = x + m @ w_out
    h = rmsnorm(x, norm2_g)
    return x + (jax.nn.silu(h @ w_ffn_gate) * (h @ w_ffn_up)) @ w_ffn_down


def setup_inputs(seed: int = 0) -> dict:
    key = jax.random.key(seed)
    ks = jax.random.split(key, 32)
    f32 = jnp.float32

    def normal(k, shape, scale=1.0):
        return jax.random.normal(k, shape, f32) * scale

    n_pages = PAST_LEN // PAGE_SIZE
    n_used = DEC_BATCH * n_pages
    n_pool = n_used + n_used // 4
    page_table = jax.random.permutation(ks[0], n_pool)[:n_used].reshape(DEC_BATCH, n_pages).astype(jnp.int32)
    a0 = jax.random.uniform(ks[1], (DEPTH, D_RNN), f32, 0.9, 0.999)
    s = a0 ** (1.0 / LRU_C)
    lru_lambda = jnp.log(s) - jnp.log1p(-s)
    return {
        'x_prompt': normal(ks[2], (BATCH, SEQ, D_MODEL)),
        'x_sample': normal(ks[3], (DEC_BATCH, DEC_SEQ, D_MODEL)),
        'cache_k': normal(ks[4], (DEPTH, n_pool, PAGE_SIZE, N_KV_HEADS, HEAD_DIM)),
        'cache_v': normal(ks[5], (DEPTH, n_pool, PAGE_SIZE, N_KV_HEADS, HEAD_DIM)),
        'cache_k_idx': normal(ks[6], (DEPTH, n_pool, PAGE_SIZE, IDX_DIM)),
        'state_conv': normal(ks[7], (DEPTH, DEC_BATCH, CONV_W - 1, D_RNN)),
        'state_h': normal(ks[8], (DEPTH, DEC_BATCH, D_RNN), 0.5),
        'page_table': page_table,
        'norm1_g': 1.0 + normal(ks[9], (DEPTH, D_MODEL), 0.02),
        'w_in': normal(ks[10], (DEPTH, D_MODEL, D_IN), D_MODEL ** -0.5),
        'conv_w': normal(ks[11], (DEPTH, CONV_W, D_RNN), CONV_W ** -0.5),
        'conv_b': normal(ks[12], (DEPTH, D_RNN), 0.01),
        'w_rg': normal(ks[13], (DEPTH, RNN_BLOCKS, RNN_BLOCK_W, RNN_BLOCK_W), RNN_BLOCK_W ** -0.5),
        'b_rg': normal(ks[14], (DEPTH, D_RNN), 0.01),
        'w_ig': normal(ks[15], (DEPTH, RNN_BLOCKS, RNN_BLOCK_W, RNN_BLOCK_W), RNN_BLOCK_W ** -0.5),
        'b_ig': normal(ks[16], (DEPTH, D_RNN), 0.01),
        'lru_lambda': lru_lambda,
        'q_norm_g': 1.0 + normal(ks[17], (DEPTH, HEAD_DIM), 0.02),
        'k_norm_g': 1.0 + normal(ks[18], (DEPTH, HEAD_DIM), 0.02),
        'k_idx_norm_g': 1.0 + normal(ks[19], (DEPTH, IDX_DIM), 0.02),
        'w_branch_a': normal(ks[20], (DEPTH, D_RNN, D_MODEL), D_RNN ** -0.5),
        'w_branch_b': normal(ks[21], (DEPTH, N_HEADS * HEAD_DIM, D_MODEL), (N_HEADS * HEAD_DIM) ** -0.5),
        'w_out': normal(ks[22], (DEPTH, D_MODEL, D_MODEL), D_MODEL ** -0.5),
        'norm2_g': 1.0 + normal(ks[23], (DEPTH, D_MODEL), 0.02),
        'w_ffn_gate': normal(ks[24], (DEPTH, D_MODEL, D_FF), D_MODEL ** -0.5),
        'w_ffn_up': normal(ks[25], (DEPTH, D_MODEL, D_FF), D_MODEL ** -0.5),
        'w_ffn_down': normal(ks[26], (DEPTH, D_FF, D_MODEL), D_FF ** -0.5),
    }


def reference(x_prompt, x_sample, cache_k, cache_v, cache_k_idx, state_conv, state_h, page_table,
              norm1_g, w_in, conv_w, conv_b, w_rg, b_rg, w_ig, b_ig, lru_lambda,
              q_norm_g, k_norm_g, k_idx_norm_g, w_branch_a, w_branch_b, w_out, norm2_g,
              w_ffn_gate, w_ffn_up, w_ffn_down):
    Bp, S, _ = x_prompt.shape
    Bd, T, _ = x_sample.shape
    past = page_table.shape[1] * PAGE_SIZE
    n_sel_p = min(TOPK_MAX, S // 4)
    n_sel_s = min(TOPK_MAX, (past + T) // 4)
    pos_p = jnp.arange(S)
    pos_s = past + jnp.arange(T)
    yp, ys = x_prompt, x_sample
    kp, vp, kip, cp, hp = [], [], [], [], []
    ks_, vs_, kis, cs, hs = [], [], [], [], []
    for l in range(DEPTH):
        xr, gr, q, k, v, qi, ki, wi, ga, gb = project(yp, pos_p, norm1_g[l], w_in[l], q_norm_g[l], k_norm_g[l], k_idx_norm_g[l])
        o_a, conv_new, h_new = rglru_branch(xr, gr, jnp.zeros((Bp, CONV_W - 1, D_RNN), yp.dtype),
                                            jnp.zeros((Bp, D_RNN), jnp.float32), conv_w[l], conv_b[l],
                                            w_rg[l], b_rg[l], w_ig[l], b_ig[l], lru_lambda[l])
        o_b = prompt_sparse_attention(q, k, v, qi, ki, wi, n_sel_p)
        yp = merge_ffn(yp, o_a, o_b, ga, gb, w_branch_a[l], w_branch_b[l], w_out[l], norm2_g[l],
                       w_ffn_gate[l], w_ffn_up[l], w_ffn_down[l])
        kp.append(k)
        vp.append(v)
        kip.append(ki)
        cp.append(conv_new)
        hp.append(h_new.astype(x_prompt.dtype))
        xr, gr, q, k, v, qi, ki, wi, ga, gb = project(ys, pos_s, norm1_g[l], w_in[l], q_norm_g[l], k_norm_g[l], k_idx_norm_g[l])
        o_a, conv_new, h_new = rglru_branch(xr, gr, state_conv[l], state_h[l], conv_w[l], conv_b[l],
                                            w_rg[l], b_rg[l], w_ig[l], b_ig[l], lru_lambda[l])
        o_b = sample_sparse_attention(q, k, v, qi, ki, wi, cache_k, cache_v, cache_k_idx, l, page_table, n_sel_s)
        ys = merge_ffn(ys, o_a, o_b, ga, gb, w_branch_a[l], w_branch_b[l], w_out[l], norm2_g[l],
                       w_ffn_gate[l], w_ffn_up[l], w_ffn_down[l])
        ks_.append(k)
        vs_.append(v)
        kis.append(ki)
        cs.append(conv_new.astype(state_conv.dtype))
        hs.append(h_new.astype(state_h.dtype))
    return (yp, ys, jnp.stack(kp), jnp.stack(vp), jnp.stack(kip), jnp.stack(cp), jnp.stack(hp),
            jnp.stack(ks_), jnp.stack(vs_), jnp.stack(kis), jnp.stack(cs), jnp.stack(hs))
```

```python
import functools

import jax
import jax.numpy as jnp
import numpy as np
from jax import lax
from jax.experimental import pallas as pl
from jax.experimental.pallas import tpu as pltpu

F32 = jnp.float32
BF16 = jnp.bfloat16

D_MODEL = 1024
D_RNN = 512
RNN_BLOCKS = 8
CONV_W = 4
LRU_C = 8.0
N_HEADS = 8
HEAD_DIM = 64
D_ATT = N_HEADS * HEAD_DIM
IDX_HEADS = 8
IDX_DIM = 64
TOPK_MAX = 256
PAGE_SIZE = 128
ROPE_THETA = 10000.0
EPS = 1e-6
LANES = 128
SUBLANES = 8
IDX_SCALE = IDX_HEADS ** -0.5 * IDX_DIM ** -0.5
ATT_SCALE = HEAD_DIM ** -0.5
NEG = -1e30
INT_MIN = -(2 ** 31)
F32_LOWEST = float(np.finfo(np.float32).min)
VMEM_LIMIT = 56 * 1024 * 1024

NT_DIMS = (((1,), (1,)), ((), ()))


def _cparams(*sem):
    return pltpu.CompilerParams(dimension_semantics=sem, vmem_limit_bytes=VMEM_LIMIT)


def _const_spec(shape):
    nd = len(shape)
    return pl.BlockSpec(shape, lambda *_: (0,) * nd, pipeline_mode=pl.Buffered(1))


def _split3(x):
    a = x.astype(BF16)
    r = x - a.astype(F32)
    b = r.astype(BF16)
    c = (r - b.astype(F32)).astype(BF16)
    return a, b, c


def _sigmoid(x):
    return 1.0 / (1.0 + jnp.exp(-x))


def _gelu_tanh(x):
    return 0.5 * x * (1.0 + jnp.tanh(np.sqrt(2.0 / np.pi) * (x + 0.044715 * (x * x * x))))


def _ordered_to_f32(o):
    bits = jnp.where(o >= 0, o, o ^ jnp.int32(0x7FFFFFFF))
    return lax.bitcast_convert_type(bits, F32)


def _proj_body(x_ref, g1_ref, wm_ref, wkw_ref, wg_ref, cs_ref, sn_ref, qg_ref, kg_ref, kig_ref, bd_ref,
               xr_ref, gr_ref, q_ref, k_ref, kb_ref, v_ref, vb_ref, qi_ref, ki_ref, kid_ref, zkw_ref,
               ga_ref, gb_ref):
    x = x_ref[...]
    ms = jnp.mean(x * x, axis=-1, keepdims=True)
    h = (x * lax.rsqrt(ms + EPS) * g1_ref[...]).astype(BF16)

    def mm(lo, hi):
        return jnp.dot(h, wm_ref[:, lo:hi], preferred_element_type=F32)

    cs = cs_ref[...]
    sn = sn_ref[...]
    lane = lax.broadcasted_iota(jnp.int32, (1, LANES), 1)
    first_half = (lane % HEAD_DIM) < (HEAD_DIM // 2)
    low_head = lane < HEAD_DIM

    def rope128(xs):
        sw = jnp.where(first_half, pltpu.roll(xs, LANES - HEAD_DIM // 2, 1), pltpu.roll(xs, HEAD_DIM // 2, 1))
        return xs * cs + sw * sn

    def rope(xn):
        return jnp.concatenate([rope128(xn[:, LANES * j:LANES * (j + 1)]) for j in range(D_ATT // LANES)], axis=1)

    bd = bd_ref[...]

    def headnorm(z, g):
        a, b, c = _split3(z * z)
        msq = (jnp.dot(a, bd, preferred_element_type=F32) + jnp.dot(b, bd, preferred_element_type=F32)
               + jnp.dot(c, bd, preferred_element_type=F32))
        return z * lax.rsqrt(msq + EPS) * g

    xr_ref[...] = mm(0, 512)
    gr_ref[...] = mm(512, 1024)
    q = rope(headnorm(mm(1024, 1536), qg_ref[...]))
    q_ref[...] = (q * ATT_SCALE).astype(BF16)
    k = rope(headnorm(mm(1536, 2048), kg_ref[...]))
    k_ref[...] = k
    kb_ref[...] = k.astype(BF16)
    v = mm(2048, 2560)
    v_ref[...] = v
    vb_ref[...] = v.astype(BF16)
    qi_ref[...] = rope(mm(2560, 3072)).astype(BF16)

    zkw = jnp.dot(h, wkw_ref[...], preferred_element_type=F32)
    zkw_ref[...] = zkw
    kms = jnp.sum(jnp.where(low_head, zkw * zkw, 0.0), axis=-1, keepdims=True) * (1.0 / IDX_DIM)
    kir = rope128(zkw * lax.rsqrt(kms + EPS) * kig_ref[...])
    ki_ref[...] = kir[:, :IDX_DIM]
    kid_ref[...] = jnp.where(low_head, kir, pltpu.roll(kir, HEAD_DIM, 1)).astype(BF16)

    ga_ref[...] = jnp.dot(h, wg_ref[:, :D_MODEL], preferred_element_type=F32)
    gb_ref[...] = jnp.dot(h, wg_ref[:, D_MODEL:], preferred_element_type=F32)


def _proj(x2d, tab_cos, tab_sin, prm, tm):
    n = x2d.shape[0]
    ntab = tab_cos.shape[0] // tm
    row = lambda i: (i, 0)
    tab = lambda i: (i % ntab, 0)
    widths = [(D_RNN, F32), (D_RNN, F32), (D_ATT, BF16), (D_ATT, F32), (D_ATT, BF16), (D_ATT, F32), (D_ATT, BF16),
              (IDX_HEADS * IDX_DIM, BF16), (IDX_DIM, F32), (LANES, BF16), (LANES, F32), (D_MODEL, F32), (D_MODEL, F32)]
    return pl.pallas_call(
        _proj_body,
        name="proj",
        grid=(n // tm,),
        in_specs=[pl.BlockSpec((tm, D_MODEL), row), _const_spec((1, D_MODEL)),
                  _const_spec(prm['w_main'].shape), _const_spec(prm['w_kw'].shape), _const_spec(prm['w_gate'].shape),
                  pl.BlockSpec((tm, LANES), tab), pl.BlockSpec((tm, LANES), tab),
                  _const_spec((1, D_ATT)), _const_spec((1, D_ATT)), _const_spec((1, LANES)),
                  _const_spec((D_ATT, D_ATT))],
        out_specs=[pl.BlockSpec((tm, w), row) for w, _ in widths],
        out_shape=[jax.ShapeDtypeStruct((n, w), dt) for w, dt in widths],
        compiler_params=_cparams("parallel"),
    )(x2d, prm['g1'], prm['w_main'], prm['w_kw'], prm['w_gate'], tab_cos, tab_sin,
      prm['qg'], prm['kg'], prm['kig'], prm['bd_head'])


def _lru_coeffs(xc, wrg_ref, brg_ref, wig_ref, big_ref, lam_ref):
    xcb = xc.astype(BF16)
    r = _sigmoid(jnp.dot(xcb, wrg_ref[...], preferred_element_type=F32) + brg_ref[...])
    i = _sigmoid(jnp.dot(xcb, wig_ref[...], preferred_element_type=F32) + big_ref[...])
    nl = -lam_ref[...]
    softplus = jnp.maximum(nl, 0.0) + jnp.log1p(jnp.exp(-jnp.abs(nl)))
    log_a = -LRU_C * r * softplus
    a = jnp.exp(log_a)
    b = jnp.sqrt(-jnp.tanh(log_a) * (a * a + 1.0)) * i * xc
    return a, b


def _rglru_seq_body(xr_ref, gr_ref, cw_ref, cb_ref, wrg_ref, brg_ref, wig_ref, big_ref, lam_ref, cbuf_ref, h0_ref,
                    oa_ref, hl_ref, prev_ref, h_ref, a_s, b_s):
    tc = xr_ref.shape[1]
    ng = tc // SUBLANES

    @pl.when(pl.program_id(1) == 0)
    def _():
        prev_ref[...] = cbuf_ref[0]
        h_ref[...] = jnp.broadcast_to(h0_ref[0], (SUBLANES, D_RNN))

    x = xr_ref[0]
    ext = jnp.concatenate([prev_ref[...], x], axis=0)
    cw = cw_ref[...]
    xc = cb_ref[...] + cw[0:1] * ext[SUBLANES - 3:SUBLANES - 3 + tc]
    xc = xc + cw[1:2] * ext[SUBLANES - 2:SUBLANES - 2 + tc]
    xc = xc + cw[2:3] * ext[SUBLANES - 1:SUBLANES - 1 + tc]
    xc = xc + cw[3:4] * x
    prev_ref[...] = x[tc - SUBLANES:tc]

    a, b = _lru_coeffs(xc, wrg_ref, brg_ref, wig_ref, big_ref, lam_ref)

    row = lax.broadcasted_iota(jnp.int32, (tc, 1), 0) % SUBLANES
    d = 1
    while d < SUBLANES:
        keep = row >= d
        a_sh = jnp.where(keep, pltpu.roll(a, d, 0), 1.0)
        b_sh = jnp.where(keep, pltpu.roll(b, d, 0), 0.0)
        b = a * b_sh + b
        a = a * a_sh
        d *= 2
    a_s[...] = a
    b_s[...] = b

    def step(g, h):
        off = pl.multiple_of(g * SUBLANES, SUBLANES)
        hr = a_s[pl.ds(off, SUBLANES), :] * h + b_s[pl.ds(off, SUBLANES), :]
        b_s[pl.ds(off, SUBLANES), :] = hr
        return jnp.broadcast_to(hr[SUBLANES - 1:SUBLANES], (SUBLANES, D_RNN))

    h = lax.fori_loop(0, ng, step, h_ref[...])
    h_ref[...] = h
    oa_ref[0] = (b_s[...] * _gelu_tanh(gr_ref[0])).astype(BF16)
    hl_ref[0] = h[0:1]


def _rglru_seq(xr, gr, cbuf8, h0, prm, tc):
    b, s, _ = xr.shape
    seq = lambda i, c: (i, c, 0)
    per_b = lambda i, c: (i, 0, 0)
    vec = _const_spec((1, D_RNN))
    return pl.pallas_call(
        _rglru_seq_body,
        name="rglru_seq",
        grid=(b, s // tc),
        in_specs=[pl.BlockSpec((1, tc, D_RNN), seq), pl.BlockSpec((1, tc, D_RNN), seq),
                  _const_spec((CONV_W, D_RNN)), vec, _const_spec((D_RNN, D_RNN)), vec,
                  _const_spec((D_RNN, D_RNN)), vec, vec,
                  pl.BlockSpec((1, SUBLANES, D_RNN), per_b), pl.BlockSpec((1, 1, D_RNN), per_b)],
        out_specs=[pl.BlockSpec((1, tc, D_RNN), seq), pl.BlockSpec((1, 1, D_RNN), per_b)],
        out_shape=[jax.ShapeDtypeStruct((b, s, D_RNN), BF16), jax.ShapeDtypeStruct((b, 1, D_RNN), F32)],
        scratch_shapes=[pltpu.VMEM((SUBLANES, D_RNN), F32), pltpu.VMEM((SUBLANES, D_RNN), F32),
                        pltpu.VMEM((tc, D_RNN), F32), pltpu.VMEM((tc, D_RNN), F32)],
        compiler_params=_cparams("parallel", "arbitrary"),
    )(xr, gr, prm['conv_w'], prm['conv_b'], prm['w_rg'], prm['b_rg'], prm['w_ig'], prm['b_ig'], prm['lam'],
      cbuf8, h0)


def _rglru_step_body(xr_ref, gr_ref, sc_ref, h0_ref, cw_ref, cb_ref, wrg_ref, brg_ref, wig_ref, big_ref, lam_ref,
                     oa_ref, h_ref):
    cw = cw_ref[...]
    x = xr_ref[...]
    xc = cb_ref[...] + cw[0:1] * sc_ref[0]
    xc = xc + cw[1:2] * sc_ref[1]
    xc = xc + cw[2:3] * sc_ref[2]
    xc = xc + cw[3:4] * x
    a, b = _lru_coeffs(xc, wrg_ref, brg_ref, wig_ref, big_ref, lam_ref)
    h = a * h0_ref[...] + b
    h_ref[...] = h
    oa_ref[...] = (h * _gelu_tanh(gr_ref[...])).astype(BF16)


def _rglru_step(xr, gr, sc_t, h0, prm):
    n = xr.shape[0]
    full = lambda shape: pl.BlockSpec(shape, lambda i: (0,) * len(shape))
    vec = full((1, D_RNN))
    return pl.pallas_call(
        _rglru_step_body,
        name="rglru_step",
        grid=(1,),
        in_specs=[full((n, D_RNN)), full((n, D_RNN)), full((CONV_W - 1, n, D_RNN)), full((n, D_RNN)),
                  full((CONV_W, D_RNN)), vec, full((D_RNN, D_RNN)), vec, full((D_RNN, D_RNN)), vec, vec],
        out_specs=[full((n, D_RNN)), full((n, D_RNN))],
        out_shape=[jax.ShapeDtypeStruct((n, D_RNN), BF16), jax.ShapeDtypeStruct((n, D_RNN), F32)],
        compiler_params=_cparams("arbitrary"),
    )(xr, gr, sc_t, h0, prm['conv_w'], prm['conv_b'], prm['w_rg'], prm['b_rg'], prm['w_ig'], prm['b_ig'], prm['lam'])


def _kth_largest_threshold(count_ge, rows, n_sel):
    def bit_step(t, r):
        cand = r + jnp.left_shift(jnp.int32(1), 31 - t)
        cnt = count_ge(_ordered_to_f32(cand))
        return jnp.where(cnt >= float(n_sel), cand, r)

    r = lax.fori_loop(0, 32, bit_step, jnp.full((rows, 1), INT_MIN, jnp.int32))
    return jnp.where(r != INT_MIN, _ordered_to_f32(r), F32_LOWEST)


def _attn_body(qi_ref, q_ref, zkw_ref, kid_ref, kb_ref, vb_ref, ob_ref,
               sc_ref, qis_ref, qs_ref, wb_ref, thr_ref, m_ref, l_ref, acc_ref, *, n_sel):
    tq = q_ref.shape[1]
    kc = tq
    npair = N_HEADS // 2
    i = pl.program_id(1)
    nch = i + 1
    lane = lax.broadcasted_iota(jnp.int32, (1, LANES), 1)
    low_head = lane < HEAD_DIM

    for j in range(npair):
        for src, dst in ((qi_ref, qis_ref), (q_ref, qs_ref)):
            blk = src[0, :, LANES * j:LANES * (j + 1)]
            zero = jnp.zeros_like(blk)
            dst[j, 0:tq, :] = jnp.where(low_head, blk, zero)
            dst[j, tq:2 * tq, :] = jnp.where(low_head, zero, blk)
    zkw = zkw_ref[0]
    for h in range(IDX_HEADS):
        wb_ref[h] = jnp.broadcast_to(zkw[:, IDX_DIM + h:IDX_DIM + h + 1], (tq, LANES))

    row = i * tq + lax.broadcasted_iota(jnp.int32, (tq, 1), 0)

    def rep(x):
        return jnp.concatenate([x] * (kc // LANES), axis=1)

    def index_chunk(c, carry):
        off = pl.multiple_of(c * kc, kc)
        kic = kid_ref[0, pl.ds(off, kc), :]
        acc = jnp.zeros((tq, kc), F32)
        for j in range(npair):
            s2 = lax.dot_general(qis_ref[j], kic, NT_DIMS, preferred_element_type=F32)
            acc = acc + rep(wb_ref[2 * j]) * jnp.maximum(s2[0:tq], 0.0)
            acc = acc + rep(wb_ref[2 * j + 1]) * jnp.maximum(s2[tq:2 * tq], 0.0)
        col = c * kc + lax.broadcasted_iota(jnp.int32, (1, kc), 1)
        sc_ref[c] = jnp.where(col <= row, acc * IDX_SCALE, -jnp.inf)
        return carry

    lax.fori_loop(0, nch, index_chunk, 0)

    def count_ge(cand):
        cb = jnp.broadcast_to(cand, (tq, LANES))

        def cnt_chunk(c, acc):
            s = sc_ref[c]
            for g in range(kc // LANES):
                acc = acc + jnp.where(s[:, LANES * g:LANES * (g + 1)] >= cb, 1.0, 0.0)
            return acc

        acc = lax.fori_loop(0, nch, cnt_chunk, jnp.zeros((tq, LANES), F32))
        return jnp.sum(acc, axis=-1, keepdims=True)

    thr = _kth_largest_threshold(count_ge, tq, n_sel)
    thr_ref[...] = jnp.broadcast_to(thr, (tq, LANES))

    m_ref[...] = jnp.full(m_ref.shape, NEG, F32)
    l_ref[...] = jnp.zeros(l_ref.shape, F32)
    acc_ref[...] = jnp.zeros(acc_ref.shape, F32)

    def attend_chunk(c, carry):
        off = pl.multiple_of(c * kc, kc)
        sel = sc_ref[c] >= rep(thr_ref[...])
        sel2 = jnp.concatenate([sel, sel], axis=0)
        for j in range(npair):
            kj = kb_ref[0, pl.ds(off, kc), LANES * j:LANES * (j + 1)]
            vj = vb_ref[0, pl.ds(off, kc), LANES * j:LANES * (j + 1)]
            s2 = lax.dot_general(qs_ref[j], kj, NT_DIMS, preferred_element_type=F32)
            s2 = jnp.where(sel2, s2, NEG)
            m_old = m_ref[j]
            m_new = jnp.maximum(m_old, jnp.max(s2, axis=-1, keepdims=True))
            alpha = jnp.exp(m_old - m_new)
            p = jnp.exp(s2 - m_new)
            l_ref[j] = alpha * l_ref[j] + jnp.sum(p, axis=-1, keepdims=True)
            m_ref[j] = m_new
            acc_ref[j] = alpha * acc_ref[j] + jnp.dot(p.astype(BF16), vj, preferred_element_type=F32)
        return carry

    lax.fori_loop(0, nch, attend_chunk, 0)

    for j in range(npair):
        o = acc_ref[j] / l_ref[j]
        ob_ref[0, :, LANES * j:LANES * (j + 1)] = jnp.where(low_head, o[0:tq], o[tq:2 * tq]).astype(BF16)


def _attn_prompt(qi, q, zkw, kid, kb, vb, n_sel, tq):
    b, s, _ = q.shape
    nq = s // tq
    blk = lambda i, j: (i, j, 0)
    per_b = lambda i, j: (i, 0, 0)
    return pl.pallas_call(
        functools.partial(_attn_body, n_sel=n_sel),
        name="attn_prompt",
        grid=(b, nq),
        in_specs=[pl.BlockSpec((1, tq, D_ATT), blk), pl.BlockSpec((1, tq, D_ATT), blk),
                  pl.BlockSpec((1, tq, LANES), blk), pl.BlockSpec((1, s, LANES), per_b),
                  pl.BlockSpec((1, s, D_ATT), per_b), pl.BlockSpec((1, s, D_ATT), per_b)],
        out_specs=pl.BlockSpec((1, tq, D_ATT), blk),
        out_shape=jax.ShapeDtypeStruct((b, s, D_ATT), BF16),
        scratch_shapes=[pltpu.VMEM((nq, tq, tq), F32),
                        pltpu.VMEM((N_HEADS // 2, 2 * tq, LANES), BF16),
                        pltpu.VMEM((N_HEADS // 2, 2 * tq, LANES), BF16),
                        pltpu.VMEM((IDX_HEADS, tq, LANES), F32),
                        pltpu.VMEM((tq, LANES), F32),
                        pltpu.VMEM((N_HEADS // 2, 2 * tq, 1), F32),
                        pltpu.VMEM((N_HEADS // 2, 2 * tq, 1), F32),
                        pltpu.VMEM((N_HEADS // 2, 2 * tq, LANES), F32)],
        compiler_params=_cparams("parallel", "arbitrary"),
    )(qi, q, zkw, kid, kb, vb)


def _sample_index_body(pt_ref, w2_ref, wrow_ref, *rest):
    del pt_ref
    *page_refs, out_ref = rest
    pages = jnp.concatenate([r[0] for r in page_refs], axis=0).astype(BF16)
    s = jnp.dot(pages, w2_ref[0], preferred_element_type=F32)
    t = jnp.maximum(s, 0.0) * wrow_ref[0]
    sub = lax.broadcasted_iota(jnp.int32, (SUBLANES, LANES), 0)
    lane = lax.broadcasted_iota(jnp.int32, (SUBLANES, LANES), 1)
    sel = jnp.where((lane // IDX_HEADS == sub) & (sub < 2), 1.0, 0.0).astype(BF16)
    acc = jnp.zeros((SUBLANES, t.shape[0]), F32)
    for part in _split3(t):
        acc = acc + lax.dot_general(sel, part, NT_DIMS, preferred_element_type=F32)
    out_ref[0] = acc * IDX_SCALE


def _sample_index(page_table, w2, wrow, cki3, pp):
    bd, npg = page_table.shape
    half = PAGE_SIZE // 2
    page_spec = lambda j: pl.BlockSpec((1, half, LANES), lambda b, c, pt: (pt[b, c * pp + j], 0, 0))
    grid_spec = pltpu.PrefetchScalarGridSpec(
        num_scalar_prefetch=1,
        grid=(bd, npg // pp),
        in_specs=[pl.BlockSpec((1, LANES, LANES), lambda b, c, pt: (b, 0, 0)),
                  pl.BlockSpec((1, 1, LANES), lambda b, c, pt: (b, 0, 0))] + [page_spec(j) for j in range(pp)],
        out_specs=pl.BlockSpec((1, SUBLANES, pp * half), lambda b, c, pt: (b, 0, c)),
    )
    return pl.pallas_call(
        _sample_index_body,
        name="sample_index",
        grid_spec=grid_spec,
        out_shape=jax.ShapeDtypeStruct((bd, SUBLANES, npg * half), F32),
        compiler_params=_cparams("parallel", "arbitrary"),
    )(page_table, w2, wrow, *([cki3] * pp))


def _sample_select_body(sc_ref, qi_ref, kid_ref, zkw_ref, bdh_ref, bias_ref, biasn_ref, *, n_sel, chunk):
    rows, past = sc_ref.shape
    lane = lax.broadcasted_iota(jnp.int32, (1, LANES), 1)
    kid = kid_ref[...].astype(F32)
    prod = qi_ref[...].astype(F32) * jnp.concatenate([kid] * (IDX_HEADS * IDX_DIM // LANES), axis=1)
    sh = jnp.zeros((rows, LANES), F32)
    for part in _split3(prod):
        sh = sh + jnp.dot(part, bdh_ref[...], preferred_element_type=F32)
    wi = pltpu.roll(zkw_ref[...], LANES - IDX_DIM, 1)
    new = jnp.sum(jnp.where(lane < IDX_HEADS, jnp.maximum(sh, 0.0) * wi, 0.0), axis=-1, keepdims=True) * IDX_SCALE

    nchunk = past // chunk

    def count_ge(cand):
        cb = jnp.broadcast_to(cand, (rows, LANES))

        def cnt_chunk(c, acc):
            off = pl.multiple_of(c * chunk, chunk)
            s = sc_ref[:, pl.ds(off, chunk)]
            for g in range(chunk // LANES):
                acc = acc + jnp.where(s[:, LANES * g:LANES * (g + 1)] >= cb, 1.0, 0.0)
            return acc

        acc = lax.fori_loop(0, nchunk, cnt_chunk, jnp.zeros((rows, LANES), F32))
        return jnp.sum(acc, axis=-1, keepdims=True) + jnp.where(new >= cand, 1.0, 0.0)

    thr = _kth_largest_threshold(count_ge, rows, n_sel)
    thr_b = jnp.broadcast_to(thr, (rows, LANES))

    def emit(c, carry):
        off = pl.multiple_of(c * chunk, chunk)
        s = sc_ref[:, pl.ds(off, chunk)]
        tb = jnp.concatenate([thr_b] * (chunk // LANES), axis=1)
        bias_ref[:, pl.ds(off, chunk)] = jnp.where(s >= tb, 0.0, NEG)
        return carry

    lax.fori_loop(0, nchunk, emit, 0)
    biasn_ref[...] = jnp.broadcast_to(jnp.where(new >= thr, 0.0, NEG), (rows, LANES))


def _sample_select(sc_nat, qi, kid, zkw, bdh, n_sel):
    bd, past = sc_nat.shape
    rows = min(bd, 128)
    chunk = min(past, 512)
    rb = lambda i: (i, 0)
    return pl.pallas_call(
        functools.partial(_sample_select_body, n_sel=n_sel, chunk=chunk),
        name="sample_select",
        grid=(bd // rows,),
        in_specs=[pl.BlockSpec((rows, past), rb), pl.BlockSpec((rows, IDX_HEADS * IDX_DIM), rb),
                  pl.BlockSpec((rows, LANES), rb), pl.BlockSpec((rows, LANES), rb),
                  pl.BlockSpec((IDX_HEADS * IDX_DIM, LANES), lambda i: (0, 0))],
        out_specs=[pl.BlockSpec((rows, past), rb), pl.BlockSpec((rows, LANES), rb)],
        out_shape=[jax.ShapeDtypeStruct((bd, past), F32), jax.ShapeDtypeStruct((bd, LANES), F32)],
        compiler_params=_cparams("parallel"),
    )(sc_nat, qi, kid, zkw, bdh)


def _sample_attend_body(pt_ref, q_ref, bias_ref, biasn_ref, kn_ref, vn_ref, *rest, pp):
    del pt_ref
    k_refs = rest[:pp]
    v_refs = rest[pp:2 * pp]
    ob_ref, m_ref, l_ref, acc_ref = rest[2 * pp:]
    c = pl.program_id(1)

    @pl.when(c == 0)
    def _():
        m_ref[...] = jnp.full(m_ref.shape, NEG, F32)
        l_ref[...] = jnp.zeros(l_ref.shape, F32)
        acc_ref[...] = jnp.zeros(acc_ref.shape, F32)

    sub = lax.broadcasted_iota(jnp.int32, (N_HEADS, D_ATT), 0)
    lane = lax.broadcasted_iota(jnp.int32, (N_HEADS, D_ATT), 1)
    own = lane // HEAD_DIM == sub
    qrow = jnp.broadcast_to(q_ref[0].astype(F32), (N_HEADS, D_ATT))
    qbd = jnp.where(own, qrow, 0.0)

    kc = jnp.concatenate([r[0] for r in k_refs], axis=0).astype(BF16)
    vc = jnp.concatenate([r[0] for r in v_refs], axis=0).astype(BF16)
    s = lax.dot_general(qbd.astype(BF16), kc, NT_DIMS, preferred_element_type=F32) + bias_ref[0]
    m_old = m_ref[...]
    m_new = jnp.maximum(m_old, jnp.max(s, axis=-1, keepdims=True))
    alpha = jnp.exp(m_old - m_new)
    p = jnp.exp(s - m_new)
    l_ref[...] = alpha * l_ref[...] + jnp.sum(p, axis=-1, keepdims=True)
    m_ref[...] = m_new
    acc_ref[...] = alpha * acc_ref[...] + jnp.dot(p.astype(BF16), vc, preferred_element_type=F32)

    @pl.when(c == pl.num_programs(1) - 1)
    def _():
        s_new = jnp.sum(qbd * kn_ref[0].astype(F32), axis=-1, keepdims=True) + biasn_ref[0][:, 0:1]
        m_old = m_ref[...]
        m_new = jnp.maximum(m_old, s_new)
        alpha = jnp.exp(m_old - m_new)
        p_new = jnp.exp(s_new - m_new)
        l = alpha * l_ref[...] + p_new
        acc = alpha * acc_ref[...] + p_new.astype(BF16).astype(F32) * vn_ref[0].astype(F32)
        ob_ref[0] = jnp.sum(jnp.where(own, acc / l, 0.0), axis=0, keepdims=True).astype(BF16)


def _sample_attend(page_table, q, bias, biasn, kn, vn, ck3, cv3, pp):
    bd, npg = page_table.shape
    per_b = lambda b, c, pt: (b, 0, 0)
    page_spec = lambda j: pl.BlockSpec((1, PAGE_SIZE, D_ATT), lambda b, c, pt: (pt[b, c * pp + j], 0, 0))
    grid_spec = pltpu.PrefetchScalarGridSpec(
        num_scalar_prefetch=1,
        grid=(bd, npg // pp),
        in_specs=[pl.BlockSpec((1, 1, D_ATT), per_b),
                  pl.BlockSpec((1, 1, pp * PAGE_SIZE), lambda b, c, pt: (b, 0, c)),
                  pl.BlockSpec((1, 1, LANES), per_b),
                  pl.BlockSpec((1, 1, D_ATT), per_b), pl.BlockSpec((1, 1, D_ATT), per_b)]
                 + [page_spec(j) for j in range(pp)] * 2,
        out_specs=pl.BlockSpec((1, 1, D_ATT), per_b),
        scratch_shapes=[pltpu.VMEM((N_HEADS, 1), F32), pltpu.VMEM((N_HEADS, 1), F32),
                        pltpu.VMEM((N_HEADS, D_ATT), F32)],
    )
    return pl.pallas_call(
        functools.partial(_sample_attend_body, pp=pp),
        name="sample_attend",
        grid_spec=grid_spec,
        out_shape=jax.ShapeDtypeStruct((bd, 1, D_ATT), BF16),
        compiler_params=_cparams("parallel", "arbitrary"),
    )(page_table, q, bias, biasn, kn, vn, *([ck3] * pp), *([cv3] * pp))


def _merge_body(x_ref, oa_ref, ob_ref, ga_ref, gb_ref, wa_ref, wb_ref, wo_ref, g2_ref, wfg_ref, wfu_ref, wfd_ref,
                y_ref, *, ff_chunk):
    ma = jnp.dot(oa_ref[...], wa_ref[...], preferred_element_type=F32)
    mb = jnp.dot(ob_ref[...], wb_ref[...], preferred_element_type=F32)
    m = _sigmoid(ga_ref[...]) * ma + _sigmoid(gb_ref[...]) * mb
    x1 = x_ref[...] + jnp.dot(m.astype(BF16), wo_ref[...], preferred_element_type=F32)
    ms = jnp.mean(x1 * x1, axis=-1, keepdims=True)
    h = (x1 * lax.rsqrt(ms + EPS) * g2_ref[...]).astype(BF16)
    d_ff = wfg_ref.shape[1]
    y = x1
    for lo in range(0, d_ff, ff_chunk):
        g = jnp.dot(h, wfg_ref[:, lo:lo + ff_chunk], preferred_element_type=F32)
        u = jnp.dot(h, wfu_ref[:, lo:lo + ff_chunk], preferred_element_type=F32)
        act = (g * _sigmoid(g) * u).astype(BF16)
        y = y + jnp.dot(act, wfd_ref[lo:lo + ff_chunk, :], preferred_element_type=F32)
    y_ref[...] = y


def _merge(x2d, oa, ob, ga, gb, prm, tm):
    n = x2d.shape[0]
    row = lambda i: (i, 0)
    d_ff = prm['w_fg'].shape[1]
    ff_chunk = 256 if d_ff % 256 == 0 else d_ff
    return pl.pallas_call(
        functools.partial(_merge_body, ff_chunk=ff_chunk),
        name="merge_ffn",
        grid=(n // tm,),
        in_specs=[pl.BlockSpec((tm, D_MODEL), row), pl.BlockSpec((tm, D_RNN), row), pl.BlockSpec((tm, D_ATT), row),
                  pl.BlockSpec((tm, D_MODEL), row), pl.BlockSpec((tm, D_MODEL), row),
                  _const_spec(prm['w_a'].shape), _const_spec(prm['w_b'].shape), _const_spec(prm['w_o'].shape),
                  _const_spec((1, D_MODEL)), _const_spec(prm['w_fg'].shape), _const_spec(prm['w_fu'].shape),
                  _const_spec(prm['w_fd'].shape)],
        out_specs=pl.BlockSpec((tm, D_MODEL), row),
        out_shape=jax.ShapeDtypeStruct((n, D_MODEL), F32),
        compiler_params=_cparams("parallel"),
    )(x2d, oa, ob, ga, gb, prm['w_a'], prm['w_b'], prm['w_o'], prm['g2'], prm['w_fg'], prm['w_fu'], prm['w_fd'])


def _rope_tables(pos):
    half = HEAD_DIM // 2
    inv = 1.0 / (ROPE_THETA ** (jnp.arange(half, dtype=F32) / half))
    ang = pos.astype(F32)[:, None] * inv[None, :]
    cos, sin = jnp.cos(ang), jnp.sin(ang)
    cos_h = jnp.concatenate([cos, cos], axis=-1)
    sin_h = jnp.concatenate([-sin, sin], axis=-1)
    return jnp.tile(cos_h, (1, LANES // HEAD_DIM)), jnp.tile(sin_h, (1, LANES // HEAD_DIM))


def _block_diag(w):
    nb, bw, _ = w.shape
    eye = jnp.eye(nb, dtype=w.dtype)
    return (eye[:, None, :, None] * w[:, :, None, :]).reshape(nb * bw, nb * bw)


def _layer_params(l, norm1_g, w_in, conv_w, conv_b, w_rg, b_rg, w_ig, b_ig, lru_lambda, q_norm_g, k_norm_g,
                  k_idx_norm_g, w_branch_a, w_branch_b, w_out, norm2_g, w_ffn_gate, w_ffn_up, w_ffn_down):
    w = w_in[l]
    o_ki = 2 * D_RNN + 3 * D_ATT + IDX_HEADS * IDX_DIM
    o_g = o_ki + IDX_DIM + IDX_HEADS
    w_kw = jnp.pad(w[:, o_ki:o_g], ((0, 0), (0, LANES - IDX_DIM - IDX_HEADS)))
    head_of = np.arange(D_ATT) // HEAD_DIM
    bd_head = jnp.asarray((head_of[:, None] == head_of[None, :]) / HEAD_DIM, BF16)
    bd_sum = jnp.asarray(head_of[:, None] == np.arange(LANES)[None, :], BF16)
    tile_h = lambda g: jnp.tile(g[l], N_HEADS)[None, :]
    return dict(
        g1=norm1_g[l][None, :], w_main=w[:, :o_ki].astype(BF16), w_kw=w_kw.astype(BF16),
        w_gate=w[:, o_g:].astype(BF16), qg=tile_h(q_norm_g), kg=tile_h(k_norm_g),
        kig=jnp.pad(k_idx_norm_g[l], (0, LANES - IDX_DIM))[None, :], bd_head=bd_head, bd_sum=bd_sum,
        conv_w=conv_w[l], conv_b=conv_b[l][None, :], w_rg=_block_diag(w_rg[l]).astype(BF16), b_rg=b_rg[l][None, :],
        w_ig=_block_diag(w_ig[l]).astype(BF16), b_ig=b_ig[l][None, :], lam=lru_lambda[l][None, :],
        w_a=w_branch_a[l].astype(BF16), w_b=w_branch_b[l].astype(BF16), w_o=w_out[l].astype(BF16),
        g2=norm2_g[l][None, :], w_fg=w_ffn_gate[l].astype(BF16), w_fu=w_ffn_up[l].astype(BF16),
        w_fd=w_ffn_down[l].astype(BF16))


def _pick_tile(n, pref):
    t = min(n, pref)
    while n % t:
        t //= 2
    return t


def _prompt_layer(x, prm):
    b, s, _ = x.shape
    n_sel = min(TOPK_MAX, s // 4)
    tm = _pick_tile(s, 256)
    cos, sin = _rope_tables(jnp.arange(s))
    xr, gr, q, k, kb, v, vb, qi, ki, kid, zkw, ga, gb = _proj(x.reshape(b * s, D_MODEL), cos, sin, prm, tm)
    r3 = lambda a: a.reshape(b, s, a.shape[-1])
    xr3 = r3(xr)
    oa, h_last = _rglru_seq(xr3, r3(gr), jnp.zeros((b, SUBLANES, D_RNN), F32), jnp.zeros((b, 1, D_RNN), F32), prm,
                            _pick_tile(s, 256))
    ob = _attn_prompt(r3(qi), r3(q), r3(zkw), r3(kid), r3(kb), r3(vb), n_sel, _pick_tile(s, 256))
    y = _merge(x.reshape(b * s, D_MODEL), oa.reshape(b * s, D_RNN), ob.reshape(b * s, D_ATT), ga, gb, prm, tm)
    xpad = jnp.concatenate([jnp.zeros((b, CONV_W - 1, D_RNN), F32), xr3], axis=1)
    return (y.reshape(b, s, D_MODEL), k.reshape(b, s, N_HEADS, HEAD_DIM), v.reshape(b, s, N_HEADS, HEAD_DIM),
            r3(ki), xpad[:, -(CONV_W - 1):], h_last.reshape(b, D_RNN))


def _sample_layer(x, cache_k, cache_v, cache_k_idx, state_conv, state_h, page_table, prm):
    bd, t, _ = x.shape
    assert t == 1
    npg = page_table.shape[1]
    past = npg * PAGE_SIZE
    n_sel = min(TOPK_MAX, (past + t) // 4)
    n_pool = cache_k.shape[0]
    cos, sin = _rope_tables(jnp.full((bd,), past))
    xr, gr, q, k, kb, v, vb, qi, ki, kid, zkw, ga, gb = _proj(x.reshape(bd, D_MODEL), cos, sin, prm, bd)
    oa, h_new = _rglru_step(xr, gr, state_conv.transpose(1, 0, 2), state_h, prm)

    qit = qi.reshape(bd, IDX_HEADS, IDX_DIM).transpose(0, 2, 1)
    zero = jnp.zeros_like(qit)
    w2 = jnp.concatenate([jnp.concatenate([qit, zero], axis=2), jnp.concatenate([zero, qit], axis=2)], axis=1)
    w2 = jnp.pad(w2, ((0, 0), (0, 0), (0, LANES - 2 * IDX_HEADS)))
    wi = zkw[:, IDX_DIM:IDX_DIM + IDX_HEADS]
    wrow = jnp.pad(jnp.concatenate([wi, wi], axis=1), ((0, 0), (0, LANES - 2 * IDX_HEADS)))[:, None, :]
    half = PAGE_SIZE // 2
    sc = _sample_index(page_table, w2, wrow, cache_k_idx.reshape(n_pool, half, LANES), _pick_tile(npg, 16))
    sc_nat = sc[:, :2, :].transpose(0, 2, 1).reshape(bd, past)
    bias, biasn = _sample_select(sc_nat, qi, kid, zkw, prm['bd_sum'], n_sel)
    ob = _sample_attend(page_table, q[:, None, :], bias[:, None, :], biasn[:, None, :], kb[:, None, :],
                        vb[:, None, :], cache_k.reshape(n_pool, PAGE_SIZE, D_ATT),
                        cache_v.reshape(n_pool, PAGE_SIZE, D_ATT), _pick_tile(npg, 8))
    y = _merge(x.reshape(bd, D_MODEL), oa, ob.reshape(bd, D_ATT), ga, gb, prm, bd)
    conv_new = jnp.concatenate([state_conv, xr[:, None, :]], axis=1)[:, -(CONV_W - 1):]
    return (y.reshape(bd, 1, D_MODEL), k.reshape(bd, 1, N_HEADS, HEAD_DIM), v.reshape(bd, 1, N_HEADS, HEAD_DIM),
            ki.reshape(bd, 1, IDX_DIM), conv_new, h_new)


def kernel(x_prompt, x_sample, cache_k, cache_v, cache_k_idx, state_conv, state_h, page_table, norm1_g, w_in, conv_w, conv_b, w_rg, b_rg, w_ig, b_ig, lru_lambda, q_norm_g, k_norm_g, k_idx_norm_g, w_branch_a, w_branch_b, w_out, norm2_g, w_ffn_gate, w_ffn_up, w_ffn_down):
    depth = w_in.shape[0]
    yp, ys = x_prompt, x_sample
    outs_p, outs_s = [], []
    for l in range(depth):
        prm = _layer_params(l, norm1_g, w_in, conv_w, conv_b, w_rg, b_rg, w_ig, b_ig, lru_lambda, q_norm_g,
                            k_norm_g, k_idx_norm_g, w_branch_a, w_branch_b, w_out, norm2_g, w_ffn_gate, w_ffn_up,
                            w_ffn_down)
        yp, *rest_p = _prompt_layer(yp, prm)
        ys, *rest_s = _sample_layer(ys, cache_k[l], cache_v[l], cache_k_idx[l], state_conv[l], state_h[l],
                                    page_table, prm)
        outs_p.append(rest_p)
        outs_s.append(rest_s)
    stack = lambda outs, i: jnp.stack([o[i] for o in outs])
    return (yp, ys, *[stack(outs_p, i) for i in range(5)], *[stack(outs_s, i) for i in range(5)])
```

```python
import functools

import jax
import jax.numpy as jnp
import numpy as np
from jax import lax
from jax.experimental import pallas as pl
from jax.experimental.pallas import tpu as pltpu

F32 = jnp.float32
BF16 = jnp.bfloat16

D_MODEL = 1024
D_RNN = 512
RNN_BLOCKS = 8
CONV_W = 4
LRU_C = 8.0
N_HEADS = 8
HEAD_DIM = 64
D_ATT = N_HEADS * HEAD_DIM
IDX_HEADS = 8
IDX_DIM = 64
TOPK_MAX = 256
PAGE_SIZE = 128
ROPE_THETA = 10000.0
EPS = 1e-6
LANES = 128
SUBLANES = 8
IDX_SCALE = IDX_HEADS ** -0.5 * IDX_DIM ** -0.5
Q_SCALE = HEAD_DIM ** -0.5 * float(np.log2(np.e))
NEG = -1e30
INT_MIN = -(2 ** 31)
F32_LOWEST = float(np.finfo(np.float32).min)
VMEM_LIMIT = 56 * 1024 * 1024

NT_DIMS = (((1,), (1,)), ((), ()))


def _cparams(*sem):
    return pltpu.CompilerParams(dimension_semantics=sem, vmem_limit_bytes=VMEM_LIMIT)


def _const_spec(shape):
    nd = len(shape)
    return pl.BlockSpec(shape, lambda *_: (0,) * nd, pipeline_mode=pl.Buffered(1))


def _split3(x):
    a = x.astype(BF16)
    r = x - a.astype(F32)
    b = r.astype(BF16)
    c = (r - b.astype(F32)).astype(BF16)
    return a, b, c


def _sigmoid(x):
    return 1.0 / (1.0 + jnp.exp(-x))


def _gelu_tanh(x):
    return 0.5 * x * (1.0 + jnp.tanh(np.sqrt(2.0 / np.pi) * (x + 0.044715 * (x * x * x))))


def _ordered_to_f32(o):
    bits = jnp.where(o >= 0, o, o ^ jnp.int32(0x7FFFFFFF))
    return lax.bitcast_convert_type(bits, F32)


def _proj_body(x_ref, g1_ref, wm_ref, wkw_ref, wg_ref, cs_ref, sn_ref, qg_ref, kg_ref, kig_ref, bd_ref,
               xr_ref, gr_ref, q_ref, k_ref, kb_ref, v_ref, vb_ref, qi_ref, ki_ref, kid_ref, zkw_ref,
               ga_ref, gb_ref):
    x = x_ref[...]
    ms = jnp.mean(x * x, axis=-1, keepdims=True)
    h = (x * lax.rsqrt(ms + EPS) * g1_ref[...]).astype(BF16)

    def mm(lo, hi):
        return jnp.dot(h, wm_ref[:, lo:hi], preferred_element_type=F32)

    cs = cs_ref[...]
    sn = sn_ref[...]
    lane = lax.broadcasted_iota(jnp.int32, (1, LANES), 1)
    first_half = (lane % HEAD_DIM) < (HEAD_DIM // 2)
    low_head = lane < HEAD_DIM

    def rope128(xs):
        sw = jnp.where(first_half, pltpu.roll(xs, LANES - HEAD_DIM // 2, 1), pltpu.roll(xs, HEAD_DIM // 2, 1))
        return xs * cs + sw * sn

    def rope(xn):
        return jnp.concatenate([rope128(xn[:, LANES * j:LANES * (j + 1)]) for j in range(D_ATT // LANES)], axis=1)

    bd = bd_ref[...]

    def headnorm(z, g):
        a, b, c = _split3(z * z)
        msq = (jnp.dot(a, bd, preferred_element_type=F32) + jnp.dot(b, bd, preferred_element_type=F32)
               + jnp.dot(c, bd, preferred_element_type=F32))
        return z * lax.rsqrt(msq + EPS) * g

    xr_ref[...] = mm(0, 512)
    gr_ref[...] = mm(512, 1024)
    q = rope(headnorm(mm(1024, 1536), qg_ref[...]))
    q_ref[...] = (q * Q_SCALE).astype(BF16)
    k = rope(headnorm(mm(1536, 2048), kg_ref[...]))
    kb_ref[...] = k.astype(BF16)
    v = mm(2048, 2560)
    vb_ref[...] = v.astype(BF16)
    for hd in range(N_HEADS):
        k_ref[:, hd, :] = k[:, HEAD_DIM * hd:HEAD_DIM * (hd + 1)]
        v_ref[:, hd, :] = v[:, HEAD_DIM * hd:HEAD_DIM * (hd + 1)]
    qi_ref[...] = rope(mm(2560, 3072)).astype(BF16)

    zkw = jnp.dot(h, wkw_ref[...], preferred_element_type=F32)
    zkw_ref[...] = zkw
    kms = jnp.sum(jnp.where(low_head, zkw * zkw, 0.0), axis=-1, keepdims=True) * (1.0 / IDX_DIM)
    kir = rope128(zkw * lax.rsqrt(kms + EPS) * kig_ref[...])
    ki_ref[...] = kir[:, :IDX_DIM]
    kid_ref[...] = jnp.where(low_head, kir, pltpu.roll(kir, HEAD_DIM, 1)).astype(BF16)

    ga_ref[...] = jnp.dot(h, wg_ref[:, :D_MODEL], preferred_element_type=F32)
    gb_ref[...] = jnp.dot(h, wg_ref[:, D_MODEL:], preferred_element_type=F32)


def _proj(x2d, tab_cos, tab_sin, prm, tm):
    n = x2d.shape[0]
    ntab = tab_cos.shape[0] // tm
    row = lambda i: (i, 0)
    tab = lambda i: (i % ntab, 0)
    heads = (N_HEADS, HEAD_DIM)
    widths = [(D_RNN, F32), (D_RNN, F32), (D_ATT, BF16), (heads, F32), (D_ATT, BF16), (heads, F32), (D_ATT, BF16),
              (IDX_HEADS * IDX_DIM, BF16), (IDX_DIM, F32), (LANES, BF16), (LANES, F32), (D_MODEL, F32), (D_MODEL, F32)]
    widths = [(w if isinstance(w, tuple) else (w,), dt) for w, dt in widths]
    return pl.pallas_call(
        _proj_body,
        name="proj",
        grid=(n // tm,),
        in_specs=[pl.BlockSpec((tm, D_MODEL), row), _const_spec((1, D_MODEL)),
                  _const_spec(prm['w_main'].shape), _const_spec(prm['w_kw'].shape), _const_spec(prm['w_gate'].shape),
                  pl.BlockSpec((tm, LANES), tab), pl.BlockSpec((tm, LANES), tab),
                  _const_spec((1, D_ATT)), _const_spec((1, D_ATT)), _const_spec((1, LANES)),
                  _const_spec((D_ATT, D_ATT))],
        out_specs=[pl.BlockSpec((tm, *w), lambda i, nd=len(w): (i,) + (0,) * nd) for w, _ in widths],
        out_shape=[jax.ShapeDtypeStruct((n, *w), dt) for w, dt in widths],
        compiler_params=_cparams("parallel"),
    )(x2d, prm['g1'], prm['w_main'], prm['w_kw'], prm['w_gate'], tab_cos, tab_sin,
      prm['qg'], prm['kg'], prm['kig'], prm['bd_head'])


def _lru_coeffs(xc, wrg_ref, brg_ref, wig_ref, big_ref, lam_ref):
    xcb = xc.astype(BF16)
    r = _sigmoid(jnp.dot(xcb, wrg_ref[...], preferred_element_type=F32) + brg_ref[...])
    i = _sigmoid(jnp.dot(xcb, wig_ref[...], preferred_element_type=F32) + big_ref[...])
    nl = -lam_ref[...]
    softplus = jnp.maximum(nl, 0.0) + jnp.log1p(jnp.exp(-jnp.abs(nl)))
    log_a = -LRU_C * r * softplus
    a = jnp.exp(log_a)
    b = jnp.sqrt(-jnp.tanh(log_a) * (a * a + 1.0)) * i * xc
    return a, b


def _rglru_seq_body(xr_ref, gr_ref, cw_ref, cb_ref, wrg_ref, brg_ref, wig_ref, big_ref, lam_ref, cbuf_ref, h0_ref,
                    oa_ref, hl_ref, prev_ref, h_ref, a_s, b_s):
    tc = xr_ref.shape[1]
    ng = tc // SUBLANES

    @pl.when(pl.program_id(1) == 0)
    def _():
        prev_ref[...] = cbuf_ref[0]
        h_ref[...] = jnp.broadcast_to(h0_ref[0], (SUBLANES, D_RNN))

    x = xr_ref[0]
    ext = jnp.concatenate([prev_ref[...], x], axis=0)
    cw = cw_ref[...]
    xc = cb_ref[...] + cw[0:1] * ext[SUBLANES - 3:SUBLANES - 3 + tc]
    xc = xc + cw[1:2] * ext[SUBLANES - 2:SUBLANES - 2 + tc]
    xc = xc + cw[2:3] * ext[SUBLANES - 1:SUBLANES - 1 + tc]
    xc = xc + cw[3:4] * x
    prev_ref[...] = x[tc - SUBLANES:tc]

    a, b = _lru_coeffs(xc, wrg_ref, brg_ref, wig_ref, big_ref, lam_ref)

    row = lax.broadcasted_iota(jnp.int32, (tc, 1), 0) % SUBLANES
    d = 1
    while d < SUBLANES:
        keep = row >= d
        a_sh = jnp.where(keep, pltpu.roll(a, d, 0), 1.0)
        b_sh = jnp.where(keep, pltpu.roll(b, d, 0), 0.0)
        b = a * b_sh + b
        a = a * a_sh
        d *= 2
    a_s[...] = a
    b_s[...] = b

    def step(g, h):
        off = pl.multiple_of(g * SUBLANES, SUBLANES)
        hr = a_s[pl.ds(off, SUBLANES), :] * h + b_s[pl.ds(off, SUBLANES), :]
        b_s[pl.ds(off, SUBLANES), :] = hr
        return jnp.broadcast_to(hr[SUBLANES - 1:SUBLANES], (SUBLANES, D_RNN))

    h = lax.fori_loop(0, ng, step, h_ref[...])
    h_ref[...] = h
    oa_ref[0] = (b_s[...] * _gelu_tanh(gr_ref[0])).astype(BF16)
    hl_ref[0] = h[0:1]


def _rglru_seq(xr, gr, cbuf8, h0, prm, tc):
    b, s, _ = xr.shape
    seq = lambda i, c: (i, c, 0)
    per_b = lambda i, c: (i, 0, 0)
    vec = _const_spec((1, D_RNN))
    return pl.pallas_call(
        _rglru_seq_body,
        name="rglru_seq",
        grid=(b, s // tc),
        in_specs=[pl.BlockSpec((1, tc, D_RNN), seq), pl.BlockSpec((1, tc, D_RNN), seq),
                  _const_spec((CONV_W, D_RNN)), vec, _const_spec((D_RNN, D_RNN)), vec,
                  _const_spec((D_RNN, D_RNN)), vec, vec,
                  pl.BlockSpec((1, SUBLANES, D_RNN), per_b), pl.BlockSpec((1, 1, D_RNN), per_b)],
        out_specs=[pl.BlockSpec((1, tc, D_RNN), seq), pl.BlockSpec((1, 1, D_RNN), per_b)],
        out_shape=[jax.ShapeDtypeStruct((b, s, D_RNN), BF16), jax.ShapeDtypeStruct((b, 1, D_RNN), F32)],
        scratch_shapes=[pltpu.VMEM((SUBLANES, D_RNN), F32), pltpu.VMEM((SUBLANES, D_RNN), F32),
                        pltpu.VMEM((tc, D_RNN), F32), pltpu.VMEM((tc, D_RNN), F32)],
        compiler_params=_cparams("parallel", "arbitrary"),
    )(xr, gr, prm['conv_w'], prm['conv_b'], prm['w_rg'], prm['b_rg'], prm['w_ig'], prm['b_ig'], prm['lam'],
      cbuf8, h0)


def _rglru_step_body(xr_ref, gr_ref, sc_ref, h0_ref, cw_ref, cb_ref, wrg_ref, brg_ref, wig_ref, big_ref, lam_ref,
                     oa_ref, h_ref):
    cw = cw_ref[...]
    x = xr_ref[...]
    xc = cb_ref[...] + cw[0:1] * sc_ref[0]
    xc = xc + cw[1:2] * sc_ref[1]
    xc = xc + cw[2:3] * sc_ref[2]
    xc = xc + cw[3:4] * x
    a, b = _lru_coeffs(xc, wrg_ref, brg_ref, wig_ref, big_ref, lam_ref)
    h = a * h0_ref[...] + b
    h_ref[...] = h
    oa_ref[...] = (h * _gelu_tanh(gr_ref[...])).astype(BF16)


def _rglru_step(xr, gr, sc_t, h0, prm):
    n = xr.shape[0]
    full = lambda shape: pl.BlockSpec(shape, lambda i: (0,) * len(shape))
    vec = full((1, D_RNN))
    return pl.pallas_call(
        _rglru_step_body,
        name="rglru_step",
        grid=(1,),
        in_specs=[full((n, D_RNN)), full((n, D_RNN)), full((CONV_W - 1, n, D_RNN)), full((n, D_RNN)),
                  full((CONV_W, D_RNN)), vec, full((D_RNN, D_RNN)), vec, full((D_RNN, D_RNN)), vec, vec],
        out_specs=[full((n, D_RNN)), full((n, D_RNN))],
        out_shape=[jax.ShapeDtypeStruct((n, D_RNN), BF16), jax.ShapeDtypeStruct((n, D_RNN), F32)],
        compiler_params=_cparams("arbitrary"),
    )(xr, gr, sc_t, h0, prm['conv_w'], prm['conv_b'], prm['w_rg'], prm['b_rg'], prm['w_ig'], prm['b_ig'], prm['lam'])


def _kth_largest_threshold(count_ge, shape, n_sel):
    def bit_step(t, carry):
        r, cnt_r = carry
        cand = r + jnp.left_shift(jnp.int32(1), 31 - t)
        cnt = count_ge(_ordered_to_f32(cand))
        keep = cnt >= float(n_sel)
        return jnp.where(keep, cand, r), jnp.where(keep, cnt, cnt_r)

    init = (jnp.full(shape, INT_MIN, jnp.int32), jnp.zeros(shape, F32))
    r, cnt_r = lax.fori_loop(0, 32, bit_step, init)
    return jnp.where(r != INT_MIN, _ordered_to_f32(r), F32_LOWEST), cnt_r


def _tie_cut(count_eq_lt, need, nbits, shape):
    def step(t, x):
        cand = x + jnp.left_shift(jnp.int32(1), nbits - 1 - t)
        return jnp.where(count_eq_lt(cand) < need, cand, x)

    return lax.fori_loop(0, nbits, step, jnp.zeros(shape, jnp.int32))


def _keep_f32(s, thr, kpos, cut):
    return jnp.where(s == thr, jnp.where(kpos <= cut, 1.0, 0.0), jnp.where(s > thr, 1.0, 0.0))


def _attn_body(qi_ref, q_ref, wt_ref, kid_ref, kb_ref, vt_ref, qg_ref, kg_ref, ob_ref,
               sc_ref, qis_ref, qs_ref, st_ref, pt_ref, m_ref, l_ref, acc_ref, *, n_sel, s_len):
    tq = q_ref.shape[1]
    kc = tq
    npair = N_HEADS // 2
    nlb = tq // LANES
    i = pl.program_id(1)
    nch = i + 1
    lane = lax.broadcasted_iota(jnp.int32, (1, LANES), 1)
    low_head = lane < HEAD_DIM

    for j in range(npair):
        for src, dst in ((qi_ref, qis_ref), (q_ref, qs_ref)):
            blk = src[0, :, LANES * j:LANES * (j + 1)]
            zero = jnp.zeros_like(blk)
            dst[j, 0:tq, :] = jnp.where(low_head, blk, zero)
            dst[j, tq:2 * tq, :] = jnp.where(low_head, zero, blk)

    qpos = i * tq + lax.broadcasted_iota(jnp.int32, (1, tq), 1)
    kiota = lax.broadcasted_iota(jnp.int32, (kc, 1), 0)
    wt = wt_ref[0]

    def index_chunk(c, carry):
        off = pl.multiple_of(c * kc, kc)
        kic = kid_ref[0, pl.ds(off, kc), :]
        for j in range(npair):
            st_ref[j] = lax.dot_general(kic, qis_ref[j], NT_DIMS, preferred_element_type=F32)
        kpos = c * kc + kiota
        for rb in range(kc // LANES):
            rows = slice(rb * LANES, (rb + 1) * LANES)
            for lb in range(nlb):
                cols = slice(lb * LANES, (lb + 1) * LANES)
                acc = jnp.zeros((LANES, LANES), F32)
                for h in range(IDX_HEADS):
                    j, half = divmod(h, 2)
                    sh = st_ref[j, rows, half * tq + lb * LANES:half * tq + (lb + 1) * LANES]
                    acc = acc + wt[h:h + 1, cols] * jnp.maximum(sh, 0.0)
                sc_ref[c, rows, cols] = jnp.where(kpos[rows] <= qpos[:, cols], acc * IDX_SCALE, -jnp.inf)
        return carry

    lax.fori_loop(0, nch, index_chunk, 0)

    def count_rows(pred):
        def chunk(c, acc):
            hit = pred(sc_ref[c], c * kc + kiota)
            return acc + jnp.sum(hit.reshape(kc // SUBLANES, SUBLANES, tq), axis=0)

        acc = lax.fori_loop(0, nch, chunk, jnp.zeros((SUBLANES, tq), F32))
        return jnp.sum(acc, axis=0, keepdims=True)

    thr, cnt = _kth_largest_threshold(
        lambda cand: count_rows(lambda s, kpos: jnp.where(s >= cand, 1.0, 0.0)), (1, tq), n_sel)
    has_ties = jnp.max(cnt) > float(n_sel)

    bound = 1.02 * HEAD_DIM * Q_SCALE * jnp.max(jnp.abs(qg_ref[...])) * jnp.max(jnp.abs(kg_ref[...]))
    fast = 2.0 * bound <= 120.0
    sel_bias = jnp.where(fast, -bound, 0.0)

    @pl.when(jnp.logical_not(has_ties))
    def _():
        def to_bias(c, carry):
            sc_ref[c] = jnp.where(sc_ref[c] >= thr, sel_bias, NEG)
            return carry

        lax.fori_loop(0, nch, to_bias, 0)

    @pl.when(has_ties)
    def _():
        need = float(n_sel) - count_rows(lambda s, kpos: jnp.where(s > thr, 1.0, 0.0))
        cut = _tie_cut(
            lambda cand: count_rows(lambda s, kpos: jnp.where(s == thr, jnp.where(kpos < cand, 1.0, 0.0), 0.0)),
            need, (s_len - 1).bit_length() + 1, (1, tq))

        def to_bias(c, carry):
            keep = _keep_f32(sc_ref[c], thr, c * kc + kiota, cut)
            sc_ref[c] = jnp.where(keep > 0.5, sel_bias, NEG)
            return carry

        lax.fori_loop(0, nch, to_bias, 0)

    m_ref[...] = jnp.full(m_ref.shape, NEG, F32)
    l_ref[...] = jnp.zeros(l_ref.shape, F32)
    acc_ref[...] = jnp.zeros(acc_ref.shape, F32)

    def attend_shifted(c, carry):
        off = pl.multiple_of(c * kc, kc)
        for j in range(npair):
            kj = kb_ref[0, pl.ds(off, kc), LANES * j:LANES * (j + 1)]
            st_ref[j] = lax.dot_general(kj, qs_ref[j], NT_DIMS, preferred_element_type=F32)
        for h in range(N_HEADS):
            j, half = divmod(h, 2)
            for lb in range(nlb):
                cols = slice(lb * LANES, (lb + 1) * LANES)
                p = jnp.exp2(st_ref[j, :, half * tq + lb * LANES:half * tq + (lb + 1) * LANES] + sc_ref[c, :, cols])
                l_ref[h:h + 1, cols] = l_ref[h:h + 1, cols] + jnp.sum(p, axis=0, keepdims=True)
                pt_ref[h, :, cols] = p.astype(BF16)
        for h in range(N_HEADS):
            hrows = slice(HEAD_DIM * h, HEAD_DIM * (h + 1))
            acc_ref[hrows, :] = acc_ref[hrows, :] + jnp.dot(vt_ref[0, c, hrows, :], pt_ref[h],
                                                            preferred_element_type=F32)
        return carry

    def attend_online(c, carry):
        off = pl.multiple_of(c * kc, kc)
        for j in range(npair):
            kj = kb_ref[0, pl.ds(off, kc), LANES * j:LANES * (j + 1)]
            st_ref[j] = lax.dot_general(kj, qs_ref[j], NT_DIMS, preferred_element_type=F32)
        for h in range(N_HEADS):
            j, half = divmod(h, 2)
            alphas = []
            for lb in range(nlb):
                cols = slice(lb * LANES, (lb + 1) * LANES)
                s = st_ref[j, :, half * tq + lb * LANES:half * tq + (lb + 1) * LANES] + sc_ref[c, :, cols]
                m_old = m_ref[h:h + 1, cols]
                m_new = jnp.maximum(m_old, jnp.max(s, axis=0, keepdims=True))
                alpha = jnp.exp2(m_old - m_new)
                p = jnp.exp2(s - m_new)
                l_ref[h:h + 1, cols] = alpha * l_ref[h:h + 1, cols] + jnp.sum(p, axis=0, keepdims=True)
                m_ref[h:h + 1, cols] = m_new
                pt_ref[h, :, cols] = p.astype(BF16)
                alphas.append(alpha)
            hrows = slice(HEAD_DIM * h, HEAD_DIM * (h + 1))
            pv = jnp.dot(vt_ref[0, c, hrows, :], pt_ref[h], preferred_element_type=F32)
            acc_ref[hrows, :] = jnp.concatenate(alphas, axis=1) * acc_ref[hrows, :] + pv
        return carry

    @pl.when(fast)
    def _():
        lax.fori_loop(0, nch, attend_shifted, 0)

    @pl.when(jnp.logical_not(fast))
    def _():
        lax.fori_loop(0, nch, attend_online, 0)

    linv = 1.0 / l_ref[...]
    for h in range(N_HEADS):
        hrows = slice(HEAD_DIM * h, HEAD_DIM * (h + 1))
        acc_ref[hrows, :] = acc_ref[hrows, :] * linv[h:h + 1]
    ob_ref[0] = acc_ref[...].T.astype(BF16)


def _attn_prompt(qi, q, wt, kid, kb, vt, qg, kg, n_sel, tq):
    b, s, _ = q.shape
    nq = s // tq
    blk = lambda i, j: (i, j, 0)
    per_b = lambda i, j: (i, 0, 0)
    gain = pl.BlockSpec((1, D_ATT), lambda i, j: (0, 0))
    return pl.pallas_call(
        functools.partial(_attn_body, n_sel=n_sel, s_len=s),
        name="attn_prompt",
        grid=(b, nq),
        in_specs=[pl.BlockSpec((1, tq, D_ATT), blk), pl.BlockSpec((1, tq, D_ATT), blk),
                  pl.BlockSpec((1, IDX_HEADS, tq), lambda i, j: (i, 0, j)), pl.BlockSpec((1, s, LANES), per_b),
                  pl.BlockSpec((1, s, D_ATT), per_b), pl.BlockSpec((1, nq, D_ATT, tq), lambda i, j: (i, 0, 0, 0)),
                  gain, gain],
        out_specs=pl.BlockSpec((1, tq, D_ATT), blk),
        out_shape=jax.ShapeDtypeStruct((b, s, D_ATT), BF16),
        scratch_shapes=[pltpu.VMEM((nq, tq, tq), F32),
                        pltpu.VMEM((N_HEADS // 2, 2 * tq, LANES), BF16),
                        pltpu.VMEM((N_HEADS // 2, 2 * tq, LANES), BF16),
                        pltpu.VMEM((N_HEADS // 2, tq, 2 * tq), F32),
                        pltpu.VMEM((N_HEADS, tq, tq), BF16),
                        pltpu.VMEM((N_HEADS, tq), F32),
                        pltpu.VMEM((N_HEADS, tq), F32),
                        pltpu.VMEM((D_ATT, tq), F32)],
        compiler_params=_cparams("parallel", "arbitrary"),
    )(qi, q, wt, kid, kb, vt, qg, kg)


def _sample_index_body(pt_ref, qh_ref, wcol_ref, *rest):
    del pt_ref
    *page_refs, out_ref = rest
    pages = jnp.concatenate([r[0] for r in page_refs], axis=0).astype(BF16)
    s = lax.dot_general(qh_ref[0], pages, NT_DIMS, preferred_element_type=F32)
    t = jnp.maximum(s, 0.0) * wcol_ref[0]
    out_ref[0] = jnp.sum(t, axis=0, keepdims=True) * IDX_SCALE


def _sample_index(page_table, qh, wcol, cki, pp):
    bd, npg = page_table.shape
    per_b = lambda b, c, pt: (b, 0, 0)
    page_spec = lambda j: pl.BlockSpec((1, PAGE_SIZE, IDX_DIM), lambda b, c, pt: (pt[b, c * pp + j], 0, 0))
    grid_spec = pltpu.PrefetchScalarGridSpec(
        num_scalar_prefetch=1,
        grid=(bd, npg // pp),
        in_specs=[pl.BlockSpec((1, IDX_HEADS, IDX_DIM), per_b), pl.BlockSpec((1, IDX_HEADS, 1), per_b)]
                 + [page_spec(j) for j in range(pp)],
        out_specs=pl.BlockSpec((1, 1, pp * PAGE_SIZE), lambda b, c, pt: (b, 0, c)),
    )
    return pl.pallas_call(
        _sample_index_body,
        name="sample_index",
        grid_spec=grid_spec,
        out_shape=jax.ShapeDtypeStruct((bd, 1, npg * PAGE_SIZE), F32),
        compiler_params=_cparams("parallel", "arbitrary"),
    )(page_table, qh, wcol, *([cki] * pp))


def _sample_select_body(sc_ref, qi_ref, kid_ref, zkw_ref, bdh_ref, tri_ref, idx_ref, cum_ref, *, n_sel, chunk):
    rows, past = sc_ref.shape
    ngrp = past // LANES
    lane = lax.broadcasted_iota(jnp.int32, (1, LANES), 1)
    kid = kid_ref[...].astype(F32)
    prod = qi_ref[...].astype(F32) * jnp.concatenate([kid] * (IDX_HEADS * IDX_DIM // LANES), axis=1)
    sh = jnp.zeros((rows, LANES), F32)
    for part in _split3(prod):
        sh = sh + jnp.dot(part, bdh_ref[...], preferred_element_type=F32)
    wi = pltpu.roll(zkw_ref[...], LANES - IDX_DIM, 1)
    new = jnp.sum(jnp.where(lane < IDX_HEADS, jnp.maximum(sh, 0.0) * wi, 0.0), axis=-1, keepdims=True) * IDX_SCALE

    nchunk = past // chunk
    ciota = lax.broadcasted_iota(jnp.int32, (1, chunk), 1)

    def count_rows(pred):
        def cnt_chunk(c, acc):
            off = pl.multiple_of(c * chunk, chunk)
            hit = pred(sc_ref[:, pl.ds(off, chunk)], off + ciota)
            for g in range(chunk // LANES):
                acc = acc + hit[:, LANES * g:LANES * (g + 1)]
            return acc

        acc = lax.fori_loop(0, nchunk, cnt_chunk, jnp.zeros((rows, LANES), F32))
        return jnp.sum(acc, axis=-1, keepdims=True)

    thr, _ = _kth_largest_threshold(
        lambda cand: count_rows(lambda s, kpos: jnp.where(s >= cand, 1.0, 0.0)) + jnp.where(new >= cand, 1.0, 0.0),
        (rows, 1), n_sel)
    need = float(n_sel) - (count_rows(lambda s, kpos: jnp.where(s > thr, 1.0, 0.0)) + jnp.where(new > thr, 1.0, 0.0))
    cut = _tie_cut(
        lambda cand: count_rows(lambda s, kpos: jnp.where(s == thr, jnp.where(kpos < cand, 1.0, 0.0), 0.0))
        + jnp.where(new == thr, jnp.where(past < cand, 1.0, 0.0), 0.0),
        need, past.bit_length() + 1, (rows, 1))

    tri = tri_ref[...]

    def prefix_chunk(c, carry):
        off = pl.multiple_of(c * chunk, chunk)
        keep = _keep_f32(sc_ref[:, pl.ds(off, chunk)], thr, off + ciota, cut)
        for g in range(chunk // LANES):
            cum_ref[c * (chunk // LANES) + g] = jnp.dot(keep[:, LANES * g:LANES * (g + 1)].astype(BF16), tri,
                                                        preferred_element_type=F32)
        return carry

    lax.fori_loop(0, nchunk, prefix_chunk, 0)

    def add_base(g, base):
        pre = cum_ref[g] + base
        cum_ref[g] = pre
        return pre[:, LANES - 1:LANES]

    lax.fori_loop(0, ngrp, add_base, jnp.zeros((rows, 1), F32))

    slot = lax.broadcasted_iota(jnp.int32, (n_sel, LANES), 0).astype(F32)

    def per_row(b, carry):
        def grp(g, acc):
            return acc + jnp.where(cum_ref[g, pl.ds(b, 1), :] <= slot, 1.0, 0.0)

        acc = lax.fori_loop(0, ngrp, grp, jnp.zeros((n_sel, LANES), F32), unroll=4)
        cnt = jnp.broadcast_to(jnp.sum(acc, axis=-1, keepdims=True), (n_sel, LANES))
        idx_ref[pl.ds(b, 1), :] = cnt.T[0:1, :].astype(jnp.int32)
        return carry

    lax.fori_loop(0, rows, per_row, 0)


def _sample_select(sc_nat, qi, kid, zkw, bdh, n_sel):
    bd, past = sc_nat.shape
    rows = min(bd, 128)
    chunk = min(past, 512)
    rb = lambda i: (i, 0)
    tri = jnp.asarray(np.triu(np.ones((LANES, LANES), np.float32)), BF16)
    return pl.pallas_call(
        functools.partial(_sample_select_body, n_sel=n_sel, chunk=chunk),
        name="sample_select",
        grid=(bd // rows,),
        in_specs=[pl.BlockSpec((rows, past), rb), pl.BlockSpec((rows, IDX_HEADS * IDX_DIM), rb),
                  pl.BlockSpec((rows, LANES), rb), pl.BlockSpec((rows, LANES), rb),
                  pl.BlockSpec((IDX_HEADS * IDX_DIM, LANES), lambda i: (0, 0)),
                  pl.BlockSpec((LANES, LANES), lambda i: (0, 0))],
        out_specs=pl.BlockSpec((rows, n_sel), rb),
        out_shape=jax.ShapeDtypeStruct((bd, n_sel), jnp.int32),
        scratch_shapes=[pltpu.VMEM((past // LANES, rows, LANES), F32)],
        compiler_params=_cparams("parallel"),
    )(sc_nat, qi, kid, zkw, bdh, tri)


def _sample_attend_body(pt_ref, idx_ref, q_ref, kn_ref, vn_ref, ck_ref, cv_ref, ob_ref, kbuf, vbuf, sem,
                        *, n_sel, past):
    b = pl.program_id(0)
    nb = pl.num_programs(0)

    def row_copies(bb, slot, j):
        key = jnp.minimum(idx_ref[bb, j], past - 1)
        phys = pt_ref[bb, jnp.right_shift(key, PAGE_SIZE.bit_length() - 1)]
        r = key & (PAGE_SIZE - 1)
        return (pltpu.make_async_copy(ck_ref.at[phys, r], kbuf.at[slot, j], sem.at[0, slot]),
                pltpu.make_async_copy(cv_ref.at[phys, r], vbuf.at[slot, j], sem.at[1, slot]))

    def issue(bb, slot):
        def body(j, carry):
            for cp in row_copies(bb, slot, j):
                cp.start()
            return carry

        lax.fori_loop(0, n_sel, body, 0, unroll=8)

    slot = b % 2

    @pl.when(b == 0)
    def _():
        issue(0, 0)

    @pl.when(b + 1 < nb)
    def _():
        issue(b + 1, 1 - slot)

    def wait_body(j, carry):
        for cp in row_copies(b, slot, j):
            cp.wait()
        return carry

    lax.fori_loop(0, n_sel, wait_body, 0, unroll=8)

    @pl.when(idx_ref[b, n_sel - 1] == past)
    def _():
        kbuf[slot, n_sel - 1] = kn_ref[0]
        vbuf[slot, n_sel - 1] = vn_ref[0]

    q = q_ref[0]
    s = jnp.sum(kbuf[slot] * q[None], axis=-1, keepdims=True)
    m = jnp.max(s, axis=0, keepdims=True)
    p = jnp.exp2(s - m)
    l = jnp.sum(p, axis=0)
    ob_ref[0] = jnp.sum(p * vbuf[slot], axis=0) / l


def _sample_attend(page_table, idx, q3, kn, vn, ck, cv, past):
    bd, n_sel = idx.shape
    per_b = lambda b, pt, ix: (b, 0, 0)
    row_spec = pl.BlockSpec((1, N_HEADS, HEAD_DIM), per_b)
    grid_spec = pltpu.PrefetchScalarGridSpec(
        num_scalar_prefetch=2,
        grid=(bd,),
        in_specs=[row_spec, row_spec, row_spec, pl.BlockSpec(memory_space=pl.ANY), pl.BlockSpec(memory_space=pl.ANY)],
        out_specs=row_spec,
        scratch_shapes=[pltpu.VMEM((2, n_sel, N_HEADS, HEAD_DIM), F32), pltpu.VMEM((2, n_sel, N_HEADS, HEAD_DIM), F32),
                        pltpu.SemaphoreType.DMA((2, 2))],
    )
    return pl.pallas_call(
        functools.partial(_sample_attend_body, n_sel=n_sel, past=past),
        name="sample_attend",
        grid_spec=grid_spec,
        out_shape=jax.ShapeDtypeStruct((bd, N_HEADS, HEAD_DIM), F32),
        compiler_params=_cparams("arbitrary"),
    )(page_table, idx, q3, kn, vn, ck, cv)


def _merge_body(x_ref, oa_ref, ob_ref, ga_ref, gb_ref, wa_ref, wb_ref, wo_ref, g2_ref, wfg_ref, wfu_ref, wfd_ref,
                y_ref, *, ff_chunk):
    ma = jnp.dot(oa_ref[...], wa_ref[...], preferred_element_type=F32)
    mb = jnp.dot(ob_ref[...], wb_ref[...], preferred_element_type=F32)
    m = _sigmoid(ga_ref[...]) * ma + _sigmoid(gb_ref[...]) * mb
    x1 = x_ref[...] + jnp.dot(m.astype(BF16), wo_ref[...], preferred_element_type=F32)
    ms = jnp.mean(x1 * x1, axis=-1, keepdims=True)
    h = (x1 * lax.rsqrt(ms + EPS) * g2_ref[...]).astype(BF16)
    d_ff = wfg_ref.shape[1]
    y = x1
    for lo in range(0, d_ff, ff_chunk):
        g = jnp.dot(h, wfg_ref[:, lo:lo + ff_chunk], preferred_element_type=F32)
        u = jnp.dot(h, wfu_ref[:, lo:lo + ff_chunk], preferred_element_type=F32)
        act = (g * _sigmoid(g) * u).astype(BF16)
        y = y + jnp.dot(act, wfd_ref[lo:lo + ff_chunk, :], preferred_element_type=F32)
    y_ref[...] = y


def _merge(x2d, oa, ob, ga, gb, prm, tm):
    n = x2d.shape[0]
    row = lambda i: (i, 0)
    d_ff = prm['w_fg'].shape[1]
    ff_chunk = 256 if d_ff % 256 == 0 else d_ff
    return pl.pallas_call(
        functools.partial(_merge_body, ff_chunk=ff_chunk),
        name="merge_ffn",
        grid=(n // tm,),
        in_specs=[pl.BlockSpec((tm, D_MODEL), row), pl.BlockSpec((tm, D_RNN), row), pl.BlockSpec((tm, D_ATT), row),
                  pl.BlockSpec((tm, D_MODEL), row), pl.BlockSpec((tm, D_MODEL), row),
                  _const_spec(prm['w_a'].shape), _const_spec(prm['w_b'].shape), _const_spec(prm['w_o'].shape),
                  _const_spec((1, D_MODEL)), _const_spec(prm['w_fg'].shape), _const_spec(prm['w_fu'].shape),
                  _const_spec(prm['w_fd'].shape)],
        out_specs=pl.BlockSpec((tm, D_MODEL), row),
        out_shape=jax.ShapeDtypeStruct((n, D_MODEL), F32),
        compiler_params=_cparams("parallel"),
    )(x2d, oa, ob, ga, gb, prm['w_a'], prm['w_b'], prm['w_o'], prm['g2'], prm['w_fg'], prm['w_fu'], prm['w_fd'])


def _rope_tables(pos):
    half = HEAD_DIM // 2
    inv = 1.0 / (ROPE_THETA ** (jnp.arange(half, dtype=F32) / half))
    ang = pos.astype(F32)[:, None] * inv[None, :]
    cos, sin = jnp.cos(ang), jnp.sin(ang)
    cos_h = jnp.concatenate([cos, cos], axis=-1)
    sin_h = jnp.concatenate([-sin, sin], axis=-1)
    return jnp.tile(cos_h, (1, LANES // HEAD_DIM)), jnp.tile(sin_h, (1, LANES // HEAD_DIM))


def _block_diag(w):
    nb, bw, _ = w.shape
    eye = jnp.eye(nb, dtype=w.dtype)
    return (eye[:, None, :, None] * w[:, :, None, :]).reshape(nb * bw, nb * bw)


def _layer_params(l, norm1_g, w_in, conv_w, conv_b, w_rg, b_rg, w_ig, b_ig, lru_lambda, q_norm_g, k_norm_g,
                  k_idx_norm_g, w_branch_a, w_branch_b, w_out, norm2_g, w_ffn_gate, w_ffn_up, w_ffn_down):
    w = w_in[l]
    o_ki = 2 * D_RNN + 3 * D_ATT + IDX_HEADS * IDX_DIM
    o_g = o_ki + IDX_DIM + IDX_HEADS
    w_kw = jnp.pad(w[:, o_ki:o_g], ((0, 0), (0, LANES - IDX_DIM - IDX_HEADS)))
    head_of = np.arange(D_ATT) // HEAD_DIM
    bd_head = jnp.asarray((head_of[:, None] == head_of[None, :]) / HEAD_DIM, BF16)
    bd_sum = jnp.asarray(head_of[:, None] == np.arange(LANES)[None, :], BF16)
    tile_h = lambda g: jnp.tile(g[l], N_HEADS)[None, :]
    return dict(
        g1=norm1_g[l][None, :], w_main=w[:, :o_ki].astype(BF16), w_kw=w_kw.astype(BF16),
        w_gate=w[:, o_g:].astype(BF16), qg=tile_h(q_norm_g), kg=tile_h(k_norm_g),
        kig=jnp.pad(k_idx_norm_g[l], (0, LANES - IDX_DIM))[None, :], bd_head=bd_head, bd_sum=bd_sum,
        conv_w=conv_w[l], conv_b=conv_b[l][None, :], w_rg=_block_diag(w_rg[l]).astype(BF16), b_rg=b_rg[l][None, :],
        w_ig=_block_diag(w_ig[l]).astype(BF16), b_ig=b_ig[l][None, :], lam=lru_lambda[l][None, :],
        w_a=w_branch_a[l].astype(BF16), w_b=w_branch_b[l].astype(BF16), w_o=w_out[l].astype(BF16),
        g2=norm2_g[l][None, :], w_fg=w_ffn_gate[l].astype(BF16), w_fu=w_ffn_up[l].astype(BF16),
        w_fd=w_ffn_down[l].astype(BF16))


def _pick_tile(n, pref):
    t = min(n, pref)
    while n % t:
        t //= 2
    return t


def _prompt_layer(x, prm):
    b, s, _ = x.shape
    n_sel = min(TOPK_MAX, s // 4)
    tm = _pick_tile(s, 256)
    tq = _pick_tile(s, 256)
    cos, sin = _rope_tables(jnp.arange(s))
    xr, gr, q, k, kb, v, vb, qi, ki, kid, zkw, ga, gb = _proj(x.reshape(b * s, D_MODEL), cos, sin, prm, tm)
    r3 = lambda a: a.reshape(b, s, a.shape[-1])
    xr3 = r3(xr)
    oa, h_last = _rglru_seq(xr3, r3(gr), jnp.zeros((b, SUBLANES, D_RNN), F32), jnp.zeros((b, 1, D_RNN), F32), prm,
                            _pick_tile(s, 256))
    wt = r3(zkw)[:, :, IDX_DIM:IDX_DIM + IDX_HEADS].transpose(0, 2, 1)
    vt = vb.reshape(b, s // tq, tq, D_ATT).transpose(0, 1, 3, 2)
    ob = _attn_prompt(r3(qi), r3(q), wt, r3(kid), r3(kb), vt, prm['qg'], prm['kg'], n_sel, tq)
    y = _merge(x.reshape(b * s, D_MODEL), oa.reshape(b * s, D_RNN), ob.reshape(b * s, D_ATT), ga, gb, prm, tm)
    xpad = jnp.concatenate([jnp.zeros((b, CONV_W - 1, D_RNN), F32), xr3], axis=1)
    return (y.reshape(b, s, D_MODEL), k.reshape(b, s, N_HEADS, HEAD_DIM), v.reshape(b, s, N_HEADS, HEAD_DIM),
            r3(ki), xpad[:, -(CONV_W - 1):], h_last.reshape(b, D_RNN))


def _sample_layer(x, cache_k, cache_v, cache_k_idx, state_conv, state_h, page_table, prm):
    bd, t, _ = x.shape
    assert t == 1
    npg = page_table.shape[1]
    past = npg * PAGE_SIZE
    n_sel = min(TOPK_MAX, (past + t) // 4)
    cos, sin = _rope_tables(jnp.full((bd,), past))
    xr, gr, q, k, kb, v, vb, qi, ki, kid, zkw, ga, gb = _proj(x.reshape(bd, D_MODEL), cos, sin, prm, bd)
    oa, h_new = _rglru_step(xr, gr, state_conv.transpose(1, 0, 2), state_h, prm)

    qh = qi.reshape(bd, IDX_HEADS, IDX_DIM)
    wcol = zkw[:, IDX_DIM:IDX_DIM + IDX_HEADS, None]
    sc = _sample_index(page_table, qh, wcol, cache_k_idx, _pick_tile(npg, 16))
    idx = _sample_select(sc.reshape(bd, past), qi, kid, zkw, prm['bd_sum'], n_sel)
    ob = _sample_attend(page_table, idx, q.astype(F32).reshape(bd, N_HEADS, HEAD_DIM), k, v, cache_k, cache_v, past)
    y = _merge(x.reshape(bd, D_MODEL), oa, ob.reshape(bd, D_ATT).astype(BF16), ga, gb, prm, bd)
    conv_new = jnp.concatenate([state_conv, xr[:, None, :]], axis=1)[:, -(CONV_W - 1):]
    return (y.reshape(bd, 1, D_MODEL), k[:, None], v[:, None], ki.reshape(bd, 1, IDX_DIM), conv_new, h_new)


def kernel(x_prompt, x_sample, cache_k, cache_v, cache_k_idx, state_conv, state_h, page_table, norm1_g, w_in, conv_w, conv_b, w_rg, b_rg, w_ig, b_ig, lru_lambda, q_norm_g, k_norm_g, k_idx_norm_g, w_branch_a, w_branch_b, w_out, norm2_g, w_ffn_gate, w_ffn_up, w_ffn_down):
    depth = w_in.shape[0]
    yp, ys = x_prompt, x_sample
    outs_p, outs_s = [], []
    for l in range(depth):
        prm = _layer_params(l, norm1_g, w_in, conv_w, conv_b, w_rg, b_rg, w_ig, b_ig, lru_lambda, q_norm_g,
                            k_norm_g, k_idx_norm_g, w_branch_a, w_branch_b, w_out, norm2_g, w_ffn_gate, w_ffn_up,
                            w_ffn_down)
        yp, *rest_p = _prompt_layer(yp, prm)
        ys, *rest_s = _sample_layer(ys, cache_k[l], cache_v[l], cache_k_idx[l], state_conv[l], state_h[l],
                                    page_table, prm)
        outs_p.append(rest_p)
        outs_s.append(rest_s)
    stack = lambda outs, i: jnp.stack([o[i] for o in outs])
    return (yp, ys, *[stack(outs_p, i) for i in range(5)], *[stack(outs_s, i) for i in range(5)])
```

```python
import functools

import jax
import jax.numpy as jnp
import numpy as np
from jax import lax
from jax.experimental import pallas as pl
from jax.experimental.pallas import tpu as pltpu

F32 = jnp.float32
BF16 = jnp.bfloat16

D_MODEL = 1024
D_RNN = 512
RNN_BLOCKS = 8
CONV_W = 4
LRU_C = 8.0
N_HEADS = 8
HEAD_DIM = 64
D_ATT = N_HEADS * HEAD_DIM
IDX_HEADS = 8
IDX_DIM = 64
TOPK_MAX = 256
PAGE_SIZE = 128
ROPE_THETA = 10000.0
EPS = 1e-6
LANES = 128
SUBLANES = 8
IDX_SCALE = IDX_HEADS ** -0.5 * IDX_DIM ** -0.5
Q_SCALE = HEAD_DIM ** -0.5 * float(np.log2(np.e))
NEG = -1e30
INT_MIN = -(2 ** 31)
F32_LOWEST = float(np.finfo(np.float32).min)
VMEM_LIMIT = 56 * 1024 * 1024

NT_DIMS = (((1,), (1,)), ((), ()))


def _cparams(*sem):
    return pltpu.CompilerParams(dimension_semantics=sem, vmem_limit_bytes=VMEM_LIMIT)


def _const_spec(shape):
    nd = len(shape)
    return pl.BlockSpec(shape, lambda *_: (0,) * nd, pipeline_mode=pl.Buffered(1))


def _split3(x):
    a = x.astype(BF16)
    r = x - a.astype(F32)
    b = r.astype(BF16)
    c = (r - b.astype(F32)).astype(BF16)
    return a, b, c


def _sigmoid(x):
    return 1.0 / (1.0 + jnp.exp(-x))


def _gelu_tanh(x):
    return 0.5 * x * (1.0 + jnp.tanh(np.sqrt(2.0 / np.pi) * (x + 0.044715 * (x * x * x))))


def _ordered_to_f32(o):
    bits = jnp.where(o >= 0, o, o ^ jnp.int32(0x7FFFFFFF))
    return lax.bitcast_convert_type(bits, F32)


def _proj_body(x_ref, g1_ref, wm_ref, wkw_ref, wg_ref, cs_ref, sn_ref, qg_ref, kg_ref, kig_ref, bd_ref,
               xr_ref, gr_ref, q_ref, kt_ref, kb_ref, vt_ref, vb_ref, vtb_ref, qi_ref, kit_ref, kid_ref, zkw_ref,
               ga_ref, gb_ref):
    x = x_ref[...]
    ms = jnp.mean(x * x, axis=-1, keepdims=True)
    h = (x * lax.rsqrt(ms + EPS) * g1_ref[...]).astype(BF16)

    def mm(lo, hi):
        return jnp.dot(h, wm_ref[:, lo:hi], preferred_element_type=F32)

    cs = cs_ref[...]
    sn = sn_ref[...]
    lane = lax.broadcasted_iota(jnp.int32, (1, LANES), 1)
    first_half = (lane % HEAD_DIM) < (HEAD_DIM // 2)
    low_head = lane < HEAD_DIM

    def rope128(xs):
        sw = jnp.where(first_half, pltpu.roll(xs, LANES - HEAD_DIM // 2, 1), pltpu.roll(xs, HEAD_DIM // 2, 1))
        return xs * cs + sw * sn

    def rope(xn):
        return jnp.concatenate([rope128(xn[:, LANES * j:LANES * (j + 1)]) for j in range(D_ATT // LANES)], axis=1)

    bd = bd_ref[...]

    def headnorm(z, g):
        a, b, c = _split3(z * z)
        msq = (jnp.dot(a, bd, preferred_element_type=F32) + jnp.dot(b, bd, preferred_element_type=F32)
               + jnp.dot(c, bd, preferred_element_type=F32))
        return z * lax.rsqrt(msq + EPS) * g

    xr_ref[...] = mm(0, 512)
    gr_ref[...] = mm(512, 1024)
    q = rope(headnorm(mm(1024, 1536), qg_ref[...]))
    q_ref[...] = (q * Q_SCALE).astype(BF16)
    k = rope(headnorm(mm(1536, 2048), kg_ref[...]))
    kb_ref[...] = k.astype(BF16)
    v = mm(2048, 2560)
    vb_ref[...] = v.astype(BF16)
    kt_ref[0] = k.T
    vt = v.T
    vt_ref[0] = vt
    vtb_ref[0, 0] = vt.astype(BF16)
    qi_ref[...] = rope(mm(2560, 3072)).astype(BF16)

    zkw = jnp.dot(h, wkw_ref[...], preferred_element_type=F32)
    zkw_ref[...] = zkw
    kms = jnp.sum(jnp.where(low_head, zkw * zkw, 0.0), axis=-1, keepdims=True) * (1.0 / IDX_DIM)
    kir = rope128(zkw * lax.rsqrt(kms + EPS) * kig_ref[...])
    kit_ref[0] = kir.T[:IDX_DIM]
    kid_ref[...] = jnp.where(low_head, kir, pltpu.roll(kir, HEAD_DIM, 1)).astype(BF16)

    ga_ref[...] = jnp.dot(h, wg_ref[:, :D_MODEL], preferred_element_type=F32)
    gb_ref[...] = jnp.dot(h, wg_ref[:, D_MODEL:], preferred_element_type=F32)


def _proj(x2d, tab_cos, tab_sin, prm, tm, seq):
    n = x2d.shape[0]
    nblk = seq // tm
    nseq = n // seq
    row = lambda i: (i, 0)
    tab = lambda i: (i % nblk, 0)
    rows_of = lambda w, dt: (pl.BlockSpec((tm, w), row), jax.ShapeDtypeStruct((n, w), dt))
    cols_of = lambda w, dt: (pl.BlockSpec((1, w, tm), lambda i: (i // nblk, 0, i % nblk)),
                             jax.ShapeDtypeStruct((nseq, w, seq), dt))
    vtb = (pl.BlockSpec((1, 1, D_ATT, tm), lambda i: (i // nblk, i % nblk, 0, 0)),
           jax.ShapeDtypeStruct((nseq, nblk, D_ATT, tm), BF16))
    outs = [rows_of(D_RNN, F32), rows_of(D_RNN, F32), rows_of(D_ATT, BF16), cols_of(D_ATT, F32), rows_of(D_ATT, BF16),
            cols_of(D_ATT, F32), rows_of(D_ATT, BF16), vtb, rows_of(IDX_HEADS * IDX_DIM, BF16), cols_of(IDX_DIM, F32),
            rows_of(LANES, BF16), rows_of(LANES, F32), rows_of(D_MODEL, F32), rows_of(D_MODEL, F32)]
    return pl.pallas_call(
        _proj_body,
        name="proj",
        grid=(n // tm,),
        in_specs=[pl.BlockSpec((tm, D_MODEL), row), _const_spec((1, D_MODEL)),
                  _const_spec(prm['w_main'].shape), _const_spec(prm['w_kw'].shape), _const_spec(prm['w_gate'].shape),
                  pl.BlockSpec((tm, LANES), tab), pl.BlockSpec((tm, LANES), tab),
                  _const_spec((1, D_ATT)), _const_spec((1, D_ATT)), _const_spec((1, LANES)),
                  _const_spec((D_ATT, D_ATT))],
        out_specs=[spec for spec, _ in outs],
        out_shape=[shape for _, shape in outs],
        compiler_params=_cparams("parallel"),
    )(x2d, prm['g1'], prm['w_main'], prm['w_kw'], prm['w_gate'], tab_cos, tab_sin,
      prm['qg'], prm['kg'], prm['kig'], prm['bd_head'])


def _lru_coeffs(xc, wrg_ref, brg_ref, wig_ref, big_ref, lam_ref):
    xcb = xc.astype(BF16)
    r = _sigmoid(jnp.dot(xcb, wrg_ref[...], preferred_element_type=F32) + brg_ref[...])
    i = _sigmoid(jnp.dot(xcb, wig_ref[...], preferred_element_type=F32) + big_ref[...])
    nl = -lam_ref[...]
    softplus = jnp.maximum(nl, 0.0) + jnp.log1p(jnp.exp(-jnp.abs(nl)))
    log_a = -LRU_C * r * softplus
    a = jnp.exp(log_a)
    b = jnp.sqrt(-jnp.tanh(log_a) * (a * a + 1.0)) * i * xc
    return a, b


def _rglru_seq_body(xr_ref, gr_ref, cw_ref, cb_ref, wrg_ref, brg_ref, wig_ref, big_ref, lam_ref, cbuf_ref, h0_ref,
                    oa_ref, hl_ref, prev_ref, h_ref, a_s, b_s):
    tc = xr_ref.shape[1]
    ng = tc // SUBLANES

    @pl.when(pl.program_id(1) == 0)
    def _():
        prev_ref[...] = cbuf_ref[0]
        h_ref[...] = jnp.broadcast_to(h0_ref[0], (SUBLANES, D_RNN))

    x = xr_ref[0]
    ext = jnp.concatenate([prev_ref[...], x], axis=0)
    cw = cw_ref[...]
    xc = cb_ref[...] + cw[0:1] * ext[SUBLANES - 3:SUBLANES - 3 + tc]
    xc = xc + cw[1:2] * ext[SUBLANES - 2:SUBLANES - 2 + tc]
    xc = xc + cw[2:3] * ext[SUBLANES - 1:SUBLANES - 1 + tc]
    xc = xc + cw[3:4] * x
    prev_ref[...] = x[tc - SUBLANES:tc]

    a, b = _lru_coeffs(xc, wrg_ref, brg_ref, wig_ref, big_ref, lam_ref)

    row = lax.broadcasted_iota(jnp.int32, (tc, 1), 0) % SUBLANES
    d = 1
    while d < SUBLANES:
        keep = row >= d
        a_sh = jnp.where(keep, pltpu.roll(a, d, 0), 1.0)
        b_sh = jnp.where(keep, pltpu.roll(b, d, 0), 0.0)
        b = a * b_sh + b
        a = a * a_sh
        d *= 2
    a_s[...] = a
    b_s[...] = b

    def step(g, h):
        off = pl.multiple_of(g * SUBLANES, SUBLANES)
        hr = a_s[pl.ds(off, SUBLANES), :] * h + b_s[pl.ds(off, SUBLANES), :]
        b_s[pl.ds(off, SUBLANES), :] = hr
        return jnp.broadcast_to(hr[SUBLANES - 1:SUBLANES], (SUBLANES, D_RNN))

    h = lax.fori_loop(0, ng, step, h_ref[...])
    h_ref[...] = h
    oa_ref[0] = (b_s[...] * _gelu_tanh(gr_ref[0])).astype(BF16)
    hl_ref[0] = h[0:1]


def _rglru_seq(xr, gr, cbuf8, h0, prm, tc):
    b, s, _ = xr.shape
    seq = lambda i, c: (i, c, 0)
    per_b = lambda i, c: (i, 0, 0)
    vec = _const_spec((1, D_RNN))
    return pl.pallas_call(
        _rglru_seq_body,
        name="rglru_seq",
        grid=(b, s // tc),
        in_specs=[pl.BlockSpec((1, tc, D_RNN), seq), pl.BlockSpec((1, tc, D_RNN), seq),
                  _const_spec((CONV_W, D_RNN)), vec, _const_spec((D_RNN, D_RNN)), vec,
                  _const_spec((D_RNN, D_RNN)), vec, vec,
                  pl.BlockSpec((1, SUBLANES, D_RNN), per_b), pl.BlockSpec((1, 1, D_RNN), per_b)],
        out_specs=[pl.BlockSpec((1, tc, D_RNN), seq), pl.BlockSpec((1, 1, D_RNN), per_b)],
        out_shape=[jax.ShapeDtypeStruct((b, s, D_RNN), BF16), jax.ShapeDtypeStruct((b, 1, D_RNN), F32)],
        scratch_shapes=[pltpu.VMEM((SUBLANES, D_RNN), F32), pltpu.VMEM((SUBLANES, D_RNN), F32),
                        pltpu.VMEM((tc, D_RNN), F32), pltpu.VMEM((tc, D_RNN), F32)],
        compiler_params=_cparams("parallel", "arbitrary"),
    )(xr, gr, prm['conv_w'], prm['conv_b'], prm['w_rg'], prm['b_rg'], prm['w_ig'], prm['b_ig'], prm['lam'],
      cbuf8, h0)


def _rglru_step_body(xr_ref, gr_ref, sc_ref, h0_ref, cw_ref, cb_ref, wrg_ref, brg_ref, wig_ref, big_ref, lam_ref,
                     oa_ref, h_ref):
    cw = cw_ref[...]
    x = xr_ref[...]
    xc = cb_ref[...] + cw[0:1] * sc_ref[0]
    xc = xc + cw[1:2] * sc_ref[1]
    xc = xc + cw[2:3] * sc_ref[2]
    xc = xc + cw[3:4] * x
    a, b = _lru_coeffs(xc, wrg_ref, brg_ref, wig_ref, big_ref, lam_ref)
    h = a * h0_ref[...] + b
    h_ref[...] = h
    oa_ref[...] = (h * _gelu_tanh(gr_ref[...])).astype(BF16)


def _rglru_step(xr, gr, sc_t, h0, prm):
    n = xr.shape[0]
    full = lambda shape: pl.BlockSpec(shape, lambda i: (0,) * len(shape))
    vec = full((1, D_RNN))
    return pl.pallas_call(
        _rglru_step_body,
        name="rglru_step",
        grid=(1,),
        in_specs=[full((n, D_RNN)), full((n, D_RNN)), full((CONV_W - 1, n, D_RNN)), full((n, D_RNN)),
                  full((CONV_W, D_RNN)), vec, full((D_RNN, D_RNN)), vec, full((D_RNN, D_RNN)), vec, vec],
        out_specs=[full((n, D_RNN)), full((n, D_RNN))],
        out_shape=[jax.ShapeDtypeStruct((n, D_RNN), BF16), jax.ShapeDtypeStruct((n, D_RNN), F32)],
        compiler_params=_cparams("arbitrary"),
    )(xr, gr, sc_t, h0, prm['conv_w'], prm['conv_b'], prm['w_rg'], prm['b_rg'], prm['w_ig'], prm['b_ig'], prm['lam'])


def _kth_largest_threshold(stages, shape, n_sel):
    carry = (jnp.full(shape, INT_MIN, jnp.int32), jnp.zeros(shape, F32))
    top = 31
    for count_ge, nbits in stages:

        def bit_step(t, carry, count_ge=count_ge, top=top):
            r, cnt_r = carry
            cand = r + jnp.left_shift(jnp.int32(1), top - t)
            cnt = count_ge(_ordered_to_f32(cand))
            keep = cnt >= float(n_sel)
            return jnp.where(keep, cand, r), jnp.where(keep, cnt, cnt_r)

        carry = lax.fori_loop(0, nbits, bit_step, carry)
        top -= nbits
    r, cnt_r = carry
    return jnp.where(r != INT_MIN, _ordered_to_f32(r), F32_LOWEST), cnt_r


def _trunc_bf16(x):
    return lax.bitcast_convert_type(lax.bitcast_convert_type(x, jnp.int32) & jnp.int32(-65536), F32)


def _tie_cut(count_eq_lt, need, nbits, shape):
    def step(t, x):
        cand = x + jnp.left_shift(jnp.int32(1), nbits - 1 - t)
        return jnp.where(count_eq_lt(cand) < need, cand, x)

    return lax.fori_loop(0, nbits, step, jnp.zeros(shape, jnp.int32))


def _keep_f32(s, thr, kpos, cut):
    return jnp.where(s == thr, jnp.where(kpos <= cut, 1.0, 0.0), jnp.where(s > thr, 1.0, 0.0))


def _attn_body(qi_ref, q_ref, wt_ref, kid_ref, kb_ref, vt_ref, qg_ref, kg_ref, ob_ref,
               sc_ref, sb_ref, qis_ref, qs_ref, st_ref, pt_ref, m_ref, l_ref, acc_ref, *, n_sel, s_len):
    tq = q_ref.shape[1]
    kc = tq
    npair = N_HEADS // 2
    nlb = tq // LANES
    i = pl.program_id(1)
    nch = i + 1
    lane = lax.broadcasted_iota(jnp.int32, (1, LANES), 1)
    low_head = lane < HEAD_DIM

    for j in range(npair):
        for src, dst in ((qi_ref, qis_ref), (q_ref, qs_ref)):
            blk = src[0, :, LANES * j:LANES * (j + 1)]
            zero = jnp.zeros_like(blk)
            dst[j, 0:tq, :] = jnp.where(low_head, blk, zero)
            dst[j, tq:2 * tq, :] = jnp.where(low_head, zero, blk)

    qpos = i * tq + lax.broadcasted_iota(jnp.int32, (1, tq), 1)
    kiota = lax.broadcasted_iota(jnp.int32, (kc, 1), 0)
    wt = wt_ref[0]

    def index_chunk(c, carry):
        off = pl.multiple_of(c * kc, kc)
        kic = kid_ref[0, pl.ds(off, kc), :]
        for j in range(npair):
            st_ref[j] = lax.dot_general(kic, qis_ref[j], NT_DIMS, preferred_element_type=F32)
        kpos = c * kc + kiota
        for rb in range(kc // LANES):
            rows = slice(rb * LANES, (rb + 1) * LANES)
            for lb in range(nlb):
                cols = slice(lb * LANES, (lb + 1) * LANES)
                acc = jnp.zeros((LANES, LANES), F32)
                for h in range(IDX_HEADS):
                    j, half = divmod(h, 2)
                    sh = st_ref[j, rows, half * tq + lb * LANES:half * tq + (lb + 1) * LANES]
                    acc = acc + wt[h:h + 1, cols] * jnp.maximum(sh, 0.0)
                val = jnp.where(kpos[rows] <= qpos[:, cols], acc * IDX_SCALE, -jnp.inf)
                sc_ref[c, rows, cols] = val
                sb_ref[c, rows, cols] = _trunc_bf16(val).astype(BF16)
        return carry

    lax.fori_loop(0, nch, index_chunk, 0)

    @pl.when(nch % 2 == 1)
    def _():
        sc_ref[nch] = jnp.full((kc, tq), -jnp.inf, F32)
        sb_ref[nch] = jnp.full((kc, tq), -jnp.inf, BF16)

    npair_chunks = (nch + 1) // 2

    def count_rows(pred):
        def chunk2(c2, acc):
            for u in range(2):
                c = 2 * c2 + u
                hit = pred(sc_ref[c], c * kc + kiota)
                acc = acc + jnp.sum(hit.reshape(kc // SUBLANES, SUBLANES, tq), axis=0)
            return acc

        acc = lax.fori_loop(0, npair_chunks, chunk2, jnp.zeros((SUBLANES, tq), F32))
        return jnp.sum(acc, axis=0, keepdims=True)

    bf16_rows = 2 * SUBLANES
    one_b = jnp.ones((bf16_rows, tq), BF16)
    zero_b = jnp.zeros((bf16_rows, tq), BF16)

    def count_ge_coarse(cand):
        cb = jnp.broadcast_to(_trunc_bf16(cand), (bf16_rows, tq)).astype(BF16)

        def chunk2(c2, acc):
            for u in range(2):
                sb = sb_ref[2 * c2 + u]
                hits = [jnp.where(sb[bf16_rows * g:bf16_rows * (g + 1)] >= cb, one_b, zero_b)
                        for g in range(kc // bf16_rows)]
                while len(hits) > 1:
                    hits = [a + b for a, b in zip(hits[0::2], hits[1::2])]
                acc = acc + hits[0]
            return acc

        acc = lax.fori_loop(0, npair_chunks, chunk2, zero_b)
        return jnp.sum(acc.astype(F32), axis=0, keepdims=True)

    def count_ge(cand):
        return count_rows(lambda s, kpos: jnp.where(s >= cand, 1.0, 0.0))

    assert (kc // bf16_rows) * s_len // kc <= 256
    thr, cnt = _kth_largest_threshold([(count_ge_coarse, 16), (count_ge, 16)], (1, tq), n_sel)
    has_ties = jnp.max(cnt) > float(n_sel)

    bound = 1.02 * HEAD_DIM * Q_SCALE * jnp.max(jnp.abs(qg_ref[...])) * jnp.max(jnp.abs(kg_ref[...]))
    fast = 2.0 * bound <= 120.0
    sel_bias = jnp.where(fast, -bound, 0.0)

    @pl.when(jnp.logical_not(has_ties))
    def _():
        def to_bias(c, carry):
            sc_ref[c] = jnp.where(sc_ref[c] >= thr, sel_bias, NEG)
            return carry

        lax.fori_loop(0, nch, to_bias, 0)

    @pl.when(has_ties)
    def _():
        need = float(n_sel) - count_rows(lambda s, kpos: jnp.where(s > thr, 1.0, 0.0))
        cut = _tie_cut(
            lambda cand: count_rows(lambda s, kpos: jnp.where(s == thr, jnp.where(kpos < cand, 1.0, 0.0), 0.0)),
            need, (s_len - 1).bit_length() + 1, (1, tq))

        def to_bias(c, carry):
            keep = _keep_f32(sc_ref[c], thr, c * kc + kiota, cut)
            sc_ref[c] = jnp.where(keep > 0.5, sel_bias, NEG)
            return carry

        lax.fori_loop(0, nch, to_bias, 0)

    m_ref[...] = jnp.full(m_ref.shape, NEG, F32)
    l_ref[...] = jnp.zeros(l_ref.shape, F32)
    acc_ref[...] = jnp.zeros(acc_ref.shape, F32)

    def attend_shifted(c, carry):
        off = pl.multiple_of(c * kc, kc)
        for j in range(npair):
            kj = kb_ref[0, pl.ds(off, kc), LANES * j:LANES * (j + 1)]
            st_ref[j] = lax.dot_general(kj, qs_ref[j], NT_DIMS, preferred_element_type=F32)
        for h in range(N_HEADS):
            j, half = divmod(h, 2)
            for lb in range(nlb):
                cols = slice(lb * LANES, (lb + 1) * LANES)
                p = jnp.exp2(st_ref[j, :, half * tq + lb * LANES:half * tq + (lb + 1) * LANES] + sc_ref[c, :, cols])
                l_ref[h:h + 1, cols] = l_ref[h:h + 1, cols] + jnp.sum(p, axis=0, keepdims=True)
                pt_ref[h, :, cols] = p.astype(BF16)
        for h in range(N_HEADS):
            hrows = slice(HEAD_DIM * h, HEAD_DIM * (h + 1))
            acc_ref[hrows, :] = acc_ref[hrows, :] + jnp.dot(vt_ref[0, c, hrows, :], pt_ref[h],
                                                            preferred_element_type=F32)
        return carry

    def attend_online(c, carry):
        off = pl.multiple_of(c * kc, kc)
        for j in range(npair):
            kj = kb_ref[0, pl.ds(off, kc), LANES * j:LANES * (j + 1)]
            st_ref[j] = lax.dot_general(kj, qs_ref[j], NT_DIMS, preferred_element_type=F32)
        for h in range(N_HEADS):
            j, half = divmod(h, 2)
            alphas = []
            for lb in range(nlb):
                cols = slice(lb * LANES, (lb + 1) * LANES)
                s = st_ref[j, :, half * tq + lb * LANES:half * tq + (lb + 1) * LANES] + sc_ref[c, :, cols]
                m_old = m_ref[h:h + 1, cols]
                m_new = jnp.maximum(m_old, jnp.max(s, axis=0, keepdims=True))
                alpha = jnp.exp2(m_old - m_new)
                p = jnp.exp2(s - m_new)
                l_ref[h:h + 1, cols] = alpha * l_ref[h:h + 1, cols] + jnp.sum(p, axis=0, keepdims=True)
                m_ref[h:h + 1, cols] = m_new
                pt_ref[h, :, cols] = p.astype(BF16)
                alphas.append(alpha)
            hrows = slice(HEAD_DIM * h, HEAD_DIM * (h + 1))
            pv = jnp.dot(vt_ref[0, c, hrows, :], pt_ref[h], preferred_element_type=F32)
            acc_ref[hrows, :] = jnp.concatenate(alphas, axis=1) * acc_ref[hrows, :] + pv
        return carry

    @pl.when(fast)
    def _():
        lax.fori_loop(0, nch, attend_shifted, 0)

    @pl.when(jnp.logical_not(fast))
    def _():
        lax.fori_loop(0, nch, attend_online, 0)

    linv = 1.0 / l_ref[...]
    for h in range(N_HEADS):
        hrows = slice(HEAD_DIM * h, HEAD_DIM * (h + 1))
        acc_ref[hrows, :] = acc_ref[hrows, :] * linv[h:h + 1]
    ob_ref[0] = acc_ref[...].T.astype(BF16)


def _attn_prompt(qi, q, wt, kid, kb, vt, qg, kg, n_sel, tq):
    b, s, _ = q.shape
    nq = s // tq
    blk = lambda i, j: (i, j, 0)
    per_b = lambda i, j: (i, 0, 0)
    gain = pl.BlockSpec((1, D_ATT), lambda i, j: (0, 0))
    return pl.pallas_call(
        functools.partial(_attn_body, n_sel=n_sel, s_len=s),
        name="attn_prompt",
        grid=(b, nq),
        in_specs=[pl.BlockSpec((1, tq, D_ATT), blk), pl.BlockSpec((1, tq, D_ATT), blk),
                  pl.BlockSpec((1, IDX_HEADS, tq), lambda i, j: (i, 0, j)), pl.BlockSpec((1, s, LANES), per_b),
                  pl.BlockSpec((1, s, D_ATT), per_b), pl.BlockSpec((1, nq, D_ATT, tq), lambda i, j: (i, 0, 0, 0)),
                  gain, gain],
        out_specs=pl.BlockSpec((1, tq, D_ATT), blk),
        out_shape=jax.ShapeDtypeStruct((b, s, D_ATT), BF16),
        scratch_shapes=[pltpu.VMEM((nq + nq % 2, tq, tq), F32),
                        pltpu.VMEM((nq + nq % 2, tq, tq), BF16),
                        pltpu.VMEM((N_HEADS // 2, 2 * tq, LANES), BF16),
                        pltpu.VMEM((N_HEADS // 2, 2 * tq, LANES), BF16),
                        pltpu.VMEM((N_HEADS // 2, tq, 2 * tq), F32),
                        pltpu.VMEM((N_HEADS, tq, tq), BF16),
                        pltpu.VMEM((N_HEADS, tq), F32),
                        pltpu.VMEM((N_HEADS, tq), F32),
                        pltpu.VMEM((D_ATT, tq), F32)],
        compiler_params=_cparams("parallel", "arbitrary"),
    )(qi, q, wt, kid, kb, vt, qg, kg)


def _sample_index_body(pt_ref, qh_ref, wcol_ref, *rest):
    del pt_ref
    *page_refs, out_ref = rest
    kt = jnp.concatenate([r[0] for r in page_refs], axis=1).astype(BF16)
    s = jnp.dot(qh_ref[0], kt, preferred_element_type=F32)
    t = jnp.maximum(s, 0.0) * wcol_ref[0]
    out_ref[0] = jnp.sum(t, axis=0, keepdims=True) * IDX_SCALE


def _sample_index(page_table, qh, wcol, cki_t, pp):
    bd, npg = page_table.shape
    per_b = lambda b, c, pt: (b, 0, 0)
    page_spec = lambda j: pl.BlockSpec((1, IDX_DIM, PAGE_SIZE), lambda b, c, pt: (pt[b, c * pp + j], 0, 0))
    grid_spec = pltpu.PrefetchScalarGridSpec(
        num_scalar_prefetch=1,
        grid=(bd, npg // pp),
        in_specs=[pl.BlockSpec((1, IDX_HEADS, IDX_DIM), per_b), pl.BlockSpec((1, IDX_HEADS, 1), per_b)]
                 + [page_spec(j) for j in range(pp)],
        out_specs=pl.BlockSpec((1, 1, pp * PAGE_SIZE), lambda b, c, pt: (b, 0, c)),
    )
    return pl.pallas_call(
        _sample_index_body,
        name="sample_index",
        grid_spec=grid_spec,
        out_shape=jax.ShapeDtypeStruct((bd, 1, npg * PAGE_SIZE), F32),
        compiler_params=_cparams("parallel", "arbitrary"),
    )(page_table, qh, wcol, *([cki_t] * pp))


def _sample_select_body(sc_ref, qi_ref, kid_ref, zkw_ref, bdh_ref, bias_ref, biasn_ref, *, n_sel, chunk):
    rows, past = sc_ref.shape
    lane = lax.broadcasted_iota(jnp.int32, (1, LANES), 1)
    kid = kid_ref[...].astype(F32)
    prod = qi_ref[...].astype(F32) * jnp.concatenate([kid] * (IDX_HEADS * IDX_DIM // LANES), axis=1)
    sh = jnp.zeros((rows, LANES), F32)
    for part in _split3(prod):
        sh = sh + jnp.dot(part, bdh_ref[...], preferred_element_type=F32)
    wi = pltpu.roll(zkw_ref[...], LANES - IDX_DIM, 1)
    new = jnp.sum(jnp.where(lane < IDX_HEADS, jnp.maximum(sh, 0.0) * wi, 0.0), axis=-1, keepdims=True) * IDX_SCALE

    nchunk = past // chunk
    ciota = lax.broadcasted_iota(jnp.int32, (1, chunk), 1)

    def count_rows(pred):
        def cnt_chunk(c, acc):
            off = pl.multiple_of(c * chunk, chunk)
            hit = pred(sc_ref[:, pl.ds(off, chunk)], off + ciota)
            for g in range(chunk // LANES):
                acc = acc + hit[:, LANES * g:LANES * (g + 1)]
            return acc

        acc = lax.fori_loop(0, nchunk, cnt_chunk, jnp.zeros((rows, LANES), F32))
        return jnp.sum(acc, axis=-1, keepdims=True)

    thr, _ = _kth_largest_threshold(
        [(lambda cand: count_rows(lambda s, kpos: jnp.where(s >= cand, 1.0, 0.0)) + jnp.where(new >= cand, 1.0, 0.0),
          32)], (rows, 1), n_sel)
    need = float(n_sel) - (count_rows(lambda s, kpos: jnp.where(s > thr, 1.0, 0.0)) + jnp.where(new > thr, 1.0, 0.0))
    cut = _tie_cut(
        lambda cand: count_rows(lambda s, kpos: jnp.where(s == thr, jnp.where(kpos < cand, 1.0, 0.0), 0.0))
        + jnp.where(new == thr, jnp.where(past < cand, 1.0, 0.0), 0.0),
        need, past.bit_length() + 1, (rows, 1))

    def emit(c, carry):
        off = pl.multiple_of(c * chunk, chunk)
        keep = _keep_f32(sc_ref[:, pl.ds(off, chunk)], thr, off + ciota, cut)
        bias_ref[:, pl.ds(off, chunk)] = jnp.where(keep > 0.5, 0.0, NEG)
        return carry

    lax.fori_loop(0, nchunk, emit, 0)
    keep_new = _keep_f32(new, thr, jnp.int32(past), cut)
    biasn_ref[...] = jnp.broadcast_to(jnp.where(keep_new > 0.5, 0.0, NEG), (rows, LANES))


def _sample_select(sc_nat, qi, kid, zkw, bdh, n_sel):
    bd, past = sc_nat.shape
    rows = min(bd, 128)
    chunk = min(past, 512)
    rb = lambda i: (i, 0)
    return pl.pallas_call(
        functools.partial(_sample_select_body, n_sel=n_sel, chunk=chunk),
        name="sample_select",
        grid=(bd // rows,),
        in_specs=[pl.BlockSpec((rows, past), rb), pl.BlockSpec((rows, IDX_HEADS * IDX_DIM), rb),
                  pl.BlockSpec((rows, LANES), rb), pl.BlockSpec((rows, LANES), rb),
                  pl.BlockSpec((IDX_HEADS * IDX_DIM, LANES), lambda i: (0, 0))],
        out_specs=[pl.BlockSpec((rows, past), rb), pl.BlockSpec((rows, LANES), rb)],
        out_shape=[jax.ShapeDtypeStruct((bd, past), F32), jax.ShapeDtypeStruct((bd, LANES), F32)],
        compiler_params=_cparams("parallel"),
    )(sc_nat, qi, kid, zkw, bdh)


def _sample_attend_body(pt_ref, q_ref, bias_ref, biasn_ref, kn_ref, vn_ref, *rest, pp):
    del pt_ref
    k_refs = rest[:pp]
    v_refs = rest[pp:2 * pp]
    ob_ref, m_ref, l_ref, acc_ref = rest[2 * pp:]
    c = pl.program_id(1)

    @pl.when(c == 0)
    def _():
        m_ref[...] = jnp.full(m_ref.shape, NEG, F32)
        l_ref[...] = jnp.zeros(l_ref.shape, F32)
        acc_ref[...] = jnp.zeros(acc_ref.shape, F32)

    sub = lax.broadcasted_iota(jnp.int32, (N_HEADS, D_ATT), 0)
    lane = lax.broadcasted_iota(jnp.int32, (N_HEADS, D_ATT), 1)
    own = lane // HEAD_DIM == sub
    qrow = jnp.broadcast_to(q_ref[0].astype(F32), (N_HEADS, D_ATT))
    qbd = jnp.where(own, qrow, 0.0)

    def pages_t(refs):
        return jnp.concatenate([r[0].reshape(D_ATT, PAGE_SIZE) for r in refs], axis=1).astype(BF16)

    s = jnp.dot(qbd.astype(BF16), pages_t(k_refs), preferred_element_type=F32) + bias_ref[0]
    m_old = m_ref[...]
    m_new = jnp.maximum(m_old, jnp.max(s, axis=-1, keepdims=True))
    alpha = jnp.exp2(m_old - m_new)
    p = jnp.exp2(s - m_new)
    l_ref[...] = alpha * l_ref[...] + jnp.sum(p, axis=-1, keepdims=True)
    m_ref[...] = m_new
    pv = lax.dot_general(p.astype(BF16), pages_t(v_refs), NT_DIMS, preferred_element_type=F32)
    acc_ref[...] = alpha * acc_ref[...] + pv

    @pl.when(c == pl.num_programs(1) - 1)
    def _():
        s_new = jnp.sum(qbd * kn_ref[0].astype(F32), axis=-1, keepdims=True) + biasn_ref[0][:, 0:1]
        m_old = m_ref[...]
        m_new = jnp.maximum(m_old, s_new)
        alpha = jnp.exp2(m_old - m_new)
        p_new = jnp.exp2(s_new - m_new)
        l = alpha * l_ref[...] + p_new
        acc = alpha * acc_ref[...] + p_new.astype(BF16).astype(F32) * vn_ref[0].astype(F32)
        ob_ref[0] = jnp.sum(jnp.where(own, acc / l, 0.0), axis=0, keepdims=True).astype(BF16)


def _sample_attend(page_table, q, bias, biasn, kn, vn, ck_t, cv_t, pp):
    bd, npg = page_table.shape
    per_b = lambda b, c, pt: (b, 0, 0)
    row_spec = pl.BlockSpec((1, 1, D_ATT), per_b)
    page_spec = lambda j: pl.BlockSpec((1, N_HEADS, HEAD_DIM, PAGE_SIZE),
                                       lambda b, c, pt: (pt[b, c * pp + j], 0, 0, 0))
    grid_spec = pltpu.PrefetchScalarGridSpec(
        num_scalar_prefetch=1,
        grid=(bd, npg // pp),
        in_specs=[row_spec, pl.BlockSpec((1, 1, pp * PAGE_SIZE), lambda b, c, pt: (b, 0, c)),
                  pl.BlockSpec((1, 1, LANES), per_b), row_spec, row_spec] + [page_spec(j) for j in range(pp)] * 2,
        out_specs=row_spec,
        scratch_shapes=[pltpu.VMEM((N_HEADS, 1), F32), pltpu.VMEM((N_HEADS, 1), F32),
                        pltpu.VMEM((N_HEADS, D_ATT), F32)],
    )
    return pl.pallas_call(
        functools.partial(_sample_attend_body, pp=pp),
        name="sample_attend",
        grid_spec=grid_spec,
        out_shape=jax.ShapeDtypeStruct((bd, 1, D_ATT), BF16),
        compiler_params=_cparams("parallel", "arbitrary"),
    )(page_table, q, bias, biasn, kn, vn, *([ck_t] * pp), *([cv_t] * pp))


def _merge_body(x_ref, oa_ref, ob_ref, ga_ref, gb_ref, wa_ref, wb_ref, wo_ref, g2_ref, wfg_ref, wfu_ref, wfd_ref,
                y_ref, *, ff_chunk):
    ma = jnp.dot(oa_ref[...], wa_ref[...], preferred_element_type=F32)
    mb = jnp.dot(ob_ref[...], wb_ref[...], preferred_element_type=F32)
    m = _sigmoid(ga_ref[...]) * ma + _sigmoid(gb_ref[...]) * mb
    x1 = x_ref[...] + jnp.dot(m.astype(BF16), wo_ref[...], preferred_element_type=F32)
    ms = jnp.mean(x1 * x1, axis=-1, keepdims=True)
    h = (x1 * lax.rsqrt(ms + EPS) * g2_ref[...]).astype(BF16)
    d_ff = wfg_ref.shape[1]
    y = x1
    for lo in range(0, d_ff, ff_chunk):
        g = jnp.dot(h, wfg_ref[:, lo:lo + ff_chunk], preferred_element_type=F32)
        u = jnp.dot(h, wfu_ref[:, lo:lo + ff_chunk], preferred_element_type=F32)
        act = (g * _sigmoid(g) * u).astype(BF16)
        y = y + jnp.dot(act, wfd_ref[lo:lo + ff_chunk, :], preferred_element_type=F32)
    y_ref[...] = y


def _merge(x2d, oa, ob, ga, gb, prm, tm):
    n = x2d.shape[0]
    row = lambda i: (i, 0)
    d_ff = prm['w_fg'].shape[1]
    ff_chunk = 256 if d_ff % 256 == 0 else d_ff
    return pl.pallas_call(
        functools.partial(_merge_body, ff_chunk=ff_chunk),
        name="merge_ffn",
        grid=(n // tm,),
        in_specs=[pl.BlockSpec((tm, D_MODEL), row), pl.BlockSpec((tm, D_RNN), row), pl.BlockSpec((tm, D_ATT), row),
                  pl.BlockSpec((tm, D_MODEL), row), pl.BlockSpec((tm, D_MODEL), row),
                  _const_spec(prm['w_a'].shape), _const_spec(prm['w_b'].shape), _const_spec(prm['w_o'].shape),
                  _const_spec((1, D_MODEL)), _const_spec(prm['w_fg'].shape), _const_spec(prm['w_fu'].shape),
                  _const_spec(prm['w_fd'].shape)],
        out_specs=pl.BlockSpec((tm, D_MODEL), row),
        out_shape=jax.ShapeDtypeStruct((n, D_MODEL), F32),
        compiler_params=_cparams("parallel"),
    )(x2d, oa, ob, ga, gb, prm['w_a'], prm['w_b'], prm['w_o'], prm['g2'], prm['w_fg'], prm['w_fu'], prm['w_fd'])


def _rope_tables(pos):
    half = HEAD_DIM // 2
    inv = 1.0 / (ROPE_THETA ** (jnp.arange(half, dtype=F32) / half))
    ang = pos.astype(F32)[:, None] * inv[None, :]
    cos, sin = jnp.cos(ang), jnp.sin(ang)
    cos_h = jnp.concatenate([cos, cos], axis=-1)
    sin_h = jnp.concatenate([-sin, sin], axis=-1)
    return jnp.tile(cos_h, (1, LANES // HEAD_DIM)), jnp.tile(sin_h, (1, LANES // HEAD_DIM))


def _block_diag(w):
    nb, bw, _ = w.shape
    eye = jnp.eye(nb, dtype=w.dtype)
    return (eye[:, None, :, None] * w[:, :, None, :]).reshape(nb * bw, nb * bw)


def _layer_params(l, norm1_g, w_in, conv_w, conv_b, w_rg, b_rg, w_ig, b_ig, lru_lambda, q_norm_g, k_norm_g,
                  k_idx_norm_g, w_branch_a, w_branch_b, w_out, norm2_g, w_ffn_gate, w_ffn_up, w_ffn_down):
    w = w_in[l]
    o_ki = 2 * D_RNN + 3 * D_ATT + IDX_HEADS * IDX_DIM
    o_g = o_ki + IDX_DIM + IDX_HEADS
    w_kw = jnp.pad(w[:, o_ki:o_g], ((0, 0), (0, LANES - IDX_DIM - IDX_HEADS)))
    head_of = np.arange(D_ATT) // HEAD_DIM
    bd_head = jnp.asarray((head_of[:, None] == head_of[None, :]) / HEAD_DIM, BF16)
    bd_sum = jnp.asarray(head_of[:, None] == np.arange(LANES)[None, :], BF16)
    tile_h = lambda g: jnp.tile(g[l], N_HEADS)[None, :]
    return dict(
        g1=norm1_g[l][None, :], w_main=w[:, :o_ki].astype(BF16), w_kw=w_kw.astype(BF16),
        w_gate=w[:, o_g:].astype(BF16), qg=tile_h(q_norm_g), kg=tile_h(k_norm_g),
        kig=jnp.pad(k_idx_norm_g[l], (0, LANES - IDX_DIM))[None, :], bd_head=bd_head, bd_sum=bd_sum,
        conv_w=conv_w[l], conv_b=conv_b[l][None, :], w_rg=_block_diag(w_rg[l]).astype(BF16), b_rg=b_rg[l][None, :],
        w_ig=_block_diag(w_ig[l]).astype(BF16), b_ig=b_ig[l][None, :], lam=lru_lambda[l][None, :],
        w_a=w_branch_a[l].astype(BF16), w_b=w_branch_b[l].astype(BF16), w_o=w_out[l].astype(BF16),
        g2=norm2_g[l][None, :], w_fg=w_ffn_gate[l].astype(BF16), w_fu=w_ffn_up[l].astype(BF16),
        w_fd=w_ffn_down[l].astype(BF16))


def _pick_tile(n, pref):
    t = min(n, pref)
    while n % t:
        t //= 2
    return t


def _prompt_layer(x, prm):
    b, s, _ = x.shape
    n_sel = min(TOPK_MAX, s // 4)
    tm = tq = _pick_tile(s, 256)
    cos, sin = _rope_tables(jnp.arange(s))
    xr, gr, q, kt, kb, vt, vb, vtb, qi, kit, kid, zkw, ga, gb = _proj(x.reshape(b * s, D_MODEL), cos, sin, prm, tm, s)
    r3 = lambda a: a.reshape(b, s, a.shape[-1])
    xr3 = r3(xr)
    oa, h_last = _rglru_seq(xr3, r3(gr), jnp.zeros((b, SUBLANES, D_RNN), F32), jnp.zeros((b, 1, D_RNN), F32), prm,
                            _pick_tile(s, 256))
    wt = r3(zkw)[:, :, IDX_DIM:IDX_DIM + IDX_HEADS].transpose(0, 2, 1)
    ob = _attn_prompt(r3(qi), r3(q), wt, r3(kid), r3(kb), vtb, prm['qg'], prm['kg'], n_sel, tq)
    y = _merge(x.reshape(b * s, D_MODEL), oa.reshape(b * s, D_RNN), ob.reshape(b * s, D_ATT), ga, gb, prm, tm)
    xpad = jnp.concatenate([jnp.zeros((b, CONV_W - 1, D_RNN), F32), xr3], axis=1)
    heads_last = lambda a: a.reshape(b, N_HEADS, HEAD_DIM, s).transpose(0, 3, 1, 2)
    return (y.reshape(b, s, D_MODEL), heads_last(kt), heads_last(vt), kit.transpose(0, 2, 1),
            xpad[:, -(CONV_W - 1):], h_last.reshape(b, D_RNN))


def _sample_layer(x, cache_k, cache_v, cache_k_idx, state_conv, state_h, page_table, prm):
    bd, t, _ = x.shape
    assert t == 1
    npg = page_table.shape[1]
    past = npg * PAGE_SIZE
    n_sel = min(TOPK_MAX, (past + t) // 4)
    cos, sin = _rope_tables(jnp.full((bd,), past))
    xr, gr, q, kt, kb, vt, vb, _, qi, kit, kid, zkw, ga, gb = _proj(x.reshape(bd, D_MODEL), cos, sin, prm, bd, bd)
    oa, h_new = _rglru_step(xr, gr, state_conv.transpose(1, 0, 2), state_h, prm)

    qh = qi.reshape(bd, IDX_HEADS, IDX_DIM)
    wcol = zkw[:, IDX_DIM:IDX_DIM + IDX_HEADS, None]
    sc = _sample_index(page_table, qh, wcol, cache_k_idx.transpose(0, 2, 1), _pick_tile(npg, 16))
    bias, biasn = _sample_select(sc.reshape(bd, past), qi, kid, zkw, prm['bd_sum'], n_sel)
    ob = _sample_attend(page_table, q[:, None, :], bias[:, None, :], biasn[:, None, :], kb[:, None, :],
                        vb[:, None, :], cache_k.transpose(0, 2, 3, 1), cache_v.transpose(0, 2, 3, 1),
                        _pick_tile(npg, 16))
    y = _merge(x.reshape(bd, D_MODEL), oa, ob.reshape(bd, D_ATT), ga, gb, prm, bd)
    conv_new = jnp.concatenate([state_conv, xr[:, None, :]], axis=1)[:, -(CONV_W - 1):]
    heads_last = lambda a: a.reshape(N_HEADS, HEAD_DIM, bd).transpose(2, 0, 1)[:, None]
    return (y.reshape(bd, 1, D_MODEL), heads_last(kt), heads_last(vt), kit[0].T[:, None, :], conv_new, h_new)


def kernel(x_prompt, x_sample, cache_k, cache_v, cache_k_idx, state_conv, state_h, page_table, norm1_g, w_in, conv_w, conv_b, w_rg, b_rg, w_ig, b_ig, lru_lambda, q_norm_g, k_norm_g, k_idx_norm_g, w_branch_a, w_branch_b, w_out, norm2_g, w_ffn_gate, w_ffn_up, w_ffn_down):
    depth = w_in.shape[0]
    yp, ys = x_prompt, x_sample
    outs_p, outs_s = [], []
    for l in range(depth):
        prm = _layer_params(l, norm1_g, w_in, conv_w, conv_b, w_rg, b_rg, w_ig, b_ig, lru_lambda, q_norm_g,
                            k_norm_g, k_idx_norm_g, w_branch_a, w_branch_b, w_out, norm2_g, w_ffn_gate, w_ffn_up,
                            w_ffn_down)
        yp, *rest_p = _prompt_layer(yp, prm)
        ys, *rest_s = _sample_layer(ys, cache_k[l], cache_v[l], cache_k_idx[l], state_conv[l], state_h[l],
                                    page_table, prm)
        outs_p.append(rest_p)
        outs_s.append(rest_s)
    stack = lambda outs, i: jnp.stack([o[i] for o in outs])
    return (yp, ys, *[stack(outs_p, i) for i in range(5)], *[stack(outs_s, i) for i in range(5)])
```

```python
import functools

import jax
import jax.numpy as jnp
import numpy as np
from jax import lax
from jax.experimental import pallas as pl
from jax.experimental.pallas import tpu as pltpu

F32 = jnp.float32
BF16 = jnp.bfloat16

D_MODEL = 1024
D_RNN = 512
RNN_BLOCKS = 8
CONV_W = 4
LRU_C = 8.0
N_HEADS = 8
HEAD_DIM = 64
D_ATT = N_HEADS * HEAD_DIM
IDX_HEADS = 8
IDX_DIM = 64
TOPK_MAX = 256
PAGE_SIZE = 128
ROPE_THETA = 10000.0
EPS = 1e-6
LANES = 128
SUBLANES = 8
IDX_SCALE = IDX_HEADS ** -0.5 * IDX_DIM ** -0.5
Q_SCALE = HEAD_DIM ** -0.5 * float(np.log2(np.e))
NEG = -1e30
INT_MIN = -(2 ** 31)
F32_LOWEST = float(np.finfo(np.float32).min)
VMEM_LIMIT = 56 * 1024 * 1024

NT_DIMS = (((1,), (1,)), ((), ()))


def _cparams(*sem):
    return pltpu.CompilerParams(dimension_semantics=sem, vmem_limit_bytes=VMEM_LIMIT)


def _const_spec(shape):
    nd = len(shape)
    return pl.BlockSpec(shape, lambda *_: (0,) * nd, pipeline_mode=pl.Buffered(1))


def _split3(x):
    a = x.astype(BF16)
    r = x - a.astype(F32)
    b = r.astype(BF16)
    c = (r - b.astype(F32)).astype(BF16)
    return a, b, c


def _sigmoid(x):
    return 1.0 / (1.0 + jnp.exp(-x))


def _gelu_tanh(x):
    return 0.5 * x * (1.0 + jnp.tanh(np.sqrt(2.0 / np.pi) * (x + 0.044715 * (x * x * x))))


def _ordered_to_f32(o):
    bits = jnp.where(o >= 0, o, o ^ jnp.int32(0x7FFFFFFF))
    return lax.bitcast_convert_type(bits, F32)


def _proj_body(x_ref, g1_ref, wm_ref, wkw_ref, wg_ref, cs_ref, sn_ref, qg_ref, kg_ref, kig_ref, bd_ref,
               xr_ref, gr_ref, q_ref, kt_ref, kb_ref, vt_ref, vb_ref, vtb_ref, qi_ref, kit_ref, kid_ref, zkw_ref,
               ga_ref, gb_ref):
    x = x_ref[...]
    ms = jnp.mean(x * x, axis=-1, keepdims=True)
    h = (x * lax.rsqrt(ms + EPS) * g1_ref[...]).astype(BF16)

    def mm(lo, hi):
        return jnp.dot(h, wm_ref[:, lo:hi], preferred_element_type=F32)

    cs = cs_ref[...]
    sn = sn_ref[...]
    lane = lax.broadcasted_iota(jnp.int32, (1, LANES), 1)
    first_half = (lane % HEAD_DIM) < (HEAD_DIM // 2)
    low_head = lane < HEAD_DIM

    def rope128(xs):
        sw = jnp.where(first_half, pltpu.roll(xs, LANES - HEAD_DIM // 2, 1), pltpu.roll(xs, HEAD_DIM // 2, 1))
        return xs * cs + sw * sn

    def rope(xn):
        return jnp.concatenate([rope128(xn[:, LANES * j:LANES * (j + 1)]) for j in range(D_ATT // LANES)], axis=1)

    bd = bd_ref[...]

    def headnorm(z, g):
        a, b, c = _split3(z * z)
        msq = (jnp.dot(a, bd, preferred_element_type=F32) + jnp.dot(b, bd, preferred_element_type=F32)
               + jnp.dot(c, bd, preferred_element_type=F32))
        return z * lax.rsqrt(msq + EPS) * g

    xr_ref[...] = mm(0, 512)
    gr_ref[...] = mm(512, 1024)
    q = rope(headnorm(mm(1024, 1536), qg_ref[...]))
    q_ref[...] = (q * Q_SCALE).astype(BF16)
    k = rope(headnorm(mm(1536, 2048), kg_ref[...]))
    kb_ref[...] = k.astype(BF16)
    v = mm(2048, 2560)
    vb_ref[...] = v.astype(BF16)
    kt_ref[0] = k.T
    vt = v.T
    vt_ref[0] = vt
    vtb_ref[0, 0] = vt.astype(BF16)
    qi_ref[...] = rope(mm(2560, 3072)).astype(BF16)

    zkw = jnp.dot(h, wkw_ref[...], preferred_element_type=F32)
    zkw_ref[...] = zkw
    kms = jnp.sum(jnp.where(low_head, zkw * zkw, 0.0), axis=-1, keepdims=True) * (1.0 / IDX_DIM)
    kir = rope128(zkw * lax.rsqrt(kms + EPS) * kig_ref[...])
    kit_ref[0] = kir.T[:IDX_DIM]
    kid_ref[...] = jnp.where(low_head, kir, pltpu.roll(kir, HEAD_DIM, 1)).astype(BF16)

    ga_ref[...] = jnp.dot(h, wg_ref[:, :D_MODEL], preferred_element_type=F32)
    gb_ref[...] = jnp.dot(h, wg_ref[:, D_MODEL:], preferred_element_type=F32)


def _proj(x2d, tab_cos, tab_sin, prm, tm, seq):
    n = x2d.shape[0]
    nblk = seq // tm
    nseq = n // seq
    row = lambda i: (i, 0)
    tab = lambda i: (i % nblk, 0)
    rows_of = lambda w, dt: (pl.BlockSpec((tm, w), row), jax.ShapeDtypeStruct((n, w), dt))
    cols_of = lambda w, dt: (pl.BlockSpec((1, w, tm), lambda i: (i // nblk, 0, i % nblk)),
                             jax.ShapeDtypeStruct((nseq, w, seq), dt))
    vtb = (pl.BlockSpec((1, 1, D_ATT, tm), lambda i: (i // nblk, i % nblk, 0, 0)),
           jax.ShapeDtypeStruct((nseq, nblk, D_ATT, tm), BF16))
    outs = [rows_of(D_RNN, F32), rows_of(D_RNN, F32), rows_of(D_ATT, BF16), cols_of(D_ATT, F32), rows_of(D_ATT, BF16),
            cols_of(D_ATT, F32), rows_of(D_ATT, BF16), vtb, rows_of(IDX_HEADS * IDX_DIM, BF16), cols_of(IDX_DIM, F32),
            rows_of(LANES, BF16), rows_of(LANES, F32), rows_of(D_MODEL, F32), rows_of(D_MODEL, F32)]
    return pl.pallas_call(
        _proj_body,
        name="proj",
        grid=(n // tm,),
        in_specs=[pl.BlockSpec((tm, D_MODEL), row), _const_spec((1, D_MODEL)),
                  _const_spec(prm['w_main'].shape), _const_spec(prm['w_kw'].shape), _const_spec(prm['w_gate'].shape),
                  pl.BlockSpec((tm, LANES), tab), pl.BlockSpec((tm, LANES), tab),
                  _const_spec((1, D_ATT)), _const_spec((1, D_ATT)), _const_spec((1, LANES)),
                  _const_spec((D_ATT, D_ATT))],
        out_specs=[spec for spec, _ in outs],
        out_shape=[shape for _, shape in outs],
        compiler_params=_cparams("parallel"),
    )(x2d, prm['g1'], prm['w_main'], prm['w_kw'], prm['w_gate'], tab_cos, tab_sin,
      prm['qg'], prm['kg'], prm['kig'], prm['bd_head'])


def _lru_coeffs(xc, wrg_ref, brg_ref, wig_ref, big_ref, lam_ref):
    xcb = xc.astype(BF16)
    r = _sigmoid(jnp.dot(xcb, wrg_ref[...], preferred_element_type=F32) + brg_ref[...])
    i = _sigmoid(jnp.dot(xcb, wig_ref[...], preferred_element_type=F32) + big_ref[...])
    nl = -lam_ref[...]
    softplus = jnp.maximum(nl, 0.0) + jnp.log1p(jnp.exp(-jnp.abs(nl)))
    log_a = -LRU_C * r * softplus
    a = jnp.exp(log_a)
    b = jnp.sqrt(-jnp.tanh(log_a) * (a * a + 1.0)) * i * xc
    return a, b


def _rglru_seq_body(xr_ref, gr_ref, cw_ref, cb_ref, wrg_ref, brg_ref, wig_ref, big_ref, lam_ref, cbuf_ref, h0_ref,
                    oa_ref, hl_ref, prev_ref, h_ref, a_s, b_s):
    tc = xr_ref.shape[1]
    ng = tc // SUBLANES

    @pl.when(pl.program_id(1) == 0)
    def _():
        prev_ref[...] = cbuf_ref[0]
        h_ref[...] = jnp.broadcast_to(h0_ref[0], (SUBLANES, D_RNN))

    x = xr_ref[0]
    ext = jnp.concatenate([prev_ref[...], x], axis=0)
    cw = cw_ref[...]
    xc = cb_ref[...] + cw[0:1] * ext[SUBLANES - 3:SUBLANES - 3 + tc]
    xc = xc + cw[1:2] * ext[SUBLANES - 2:SUBLANES - 2 + tc]
    xc = xc + cw[2:3] * ext[SUBLANES - 1:SUBLANES - 1 + tc]
    xc = xc + cw[3:4] * x
    prev_ref[...] = x[tc - SUBLANES:tc]

    a, b = _lru_coeffs(xc, wrg_ref, brg_ref, wig_ref, big_ref, lam_ref)

    row = lax.broadcasted_iota(jnp.int32, (tc, 1), 0) % SUBLANES
    d = 1
    while d < SUBLANES:
        keep = row >= d
        a_sh = jnp.where(keep, pltpu.roll(a, d, 0), 1.0)
        b_sh = jnp.where(keep, pltpu.roll(b, d, 0), 0.0)
        b = a * b_sh + b
        a = a * a_sh
        d *= 2
    a_s[...] = a
    b_s[...] = b

    def step(g, h):
        off = pl.multiple_of(g * SUBLANES, SUBLANES)
        hr = a_s[pl.ds(off, SUBLANES), :] * h + b_s[pl.ds(off, SUBLANES), :]
        b_s[pl.ds(off, SUBLANES), :] = hr
        return jnp.broadcast_to(hr[SUBLANES - 1:SUBLANES], (SUBLANES, D_RNN))

    h = lax.fori_loop(0, ng, step, h_ref[...])
    h_ref[...] = h
    oa_ref[0] = (b_s[...] * _gelu_tanh(gr_ref[0])).astype(BF16)
    hl_ref[0] = h[0:1]


def _rglru_seq(xr, gr, cbuf8, h0, prm, tc):
    b, s, _ = xr.shape
    seq = lambda i, c: (i, c, 0)
    per_b = lambda i, c: (i, 0, 0)
    vec = _const_spec((1, D_RNN))
    return pl.pallas_call(
        _rglru_seq_body,
        name="rglru_seq",
        grid=(b, s // tc),
        in_specs=[pl.BlockSpec((1, tc, D_RNN), seq), pl.BlockSpec((1, tc, D_RNN), seq),
                  _const_spec((CONV_W, D_RNN)), vec, _const_spec((D_RNN, D_RNN)), vec,
                  _const_spec((D_RNN, D_RNN)), vec, vec,
                  pl.BlockSpec((1, SUBLANES, D_RNN), per_b), pl.BlockSpec((1, 1, D_RNN), per_b)],
        out_specs=[pl.BlockSpec((1, tc, D_RNN), seq), pl.BlockSpec((1, 1, D_RNN), per_b)],
        out_shape=[jax.ShapeDtypeStruct((b, s, D_RNN), BF16), jax.ShapeDtypeStruct((b, 1, D_RNN), F32)],
        scratch_shapes=[pltpu.VMEM((SUBLANES, D_RNN), F32), pltpu.VMEM((SUBLANES, D_RNN), F32),
                        pltpu.VMEM((tc, D_RNN), F32), pltpu.VMEM((tc, D_RNN), F32)],
        compiler_params=_cparams("parallel", "arbitrary"),
    )(xr, gr, prm['conv_w'], prm['conv_b'], prm['w_rg'], prm['b_rg'], prm['w_ig'], prm['b_ig'], prm['lam'],
      cbuf8, h0)


def _rglru_step_body(xr_ref, gr_ref, sc_ref, h0_ref, cw_ref, cb_ref, wrg_ref, brg_ref, wig_ref, big_ref, lam_ref,
                     oa_ref, h_ref):
    cw = cw_ref[...]
    x = xr_ref[...]
    xc = cb_ref[...] + cw[0:1] * sc_ref[0]
    xc = xc + cw[1:2] * sc_ref[1]
    xc = xc + cw[2:3] * sc_ref[2]
    xc = xc + cw[3:4] * x
    a, b = _lru_coeffs(xc, wrg_ref, brg_ref, wig_ref, big_ref, lam_ref)
    h = a * h0_ref[...] + b
    h_ref[...] = h
    oa_ref[...] = (h * _gelu_tanh(gr_ref[...])).astype(BF16)


def _rglru_step(xr, gr, sc_t, h0, prm):
    n = xr.shape[0]
    full = lambda shape: pl.BlockSpec(shape, lambda i: (0,) * len(shape))
    vec = full((1, D_RNN))
    return pl.pallas_call(
        _rglru_step_body,
        name="rglru_step",
        grid=(1,),
        in_specs=[full((n, D_RNN)), full((n, D_RNN)), full((CONV_W - 1, n, D_RNN)), full((n, D_RNN)),
                  full((CONV_W, D_RNN)), vec, full((D_RNN, D_RNN)), vec, full((D_RNN, D_RNN)), vec, vec],
        out_specs=[full((n, D_RNN)), full((n, D_RNN))],
        out_shape=[jax.ShapeDtypeStruct((n, D_RNN), BF16), jax.ShapeDtypeStruct((n, D_RNN), F32)],
        compiler_params=_cparams("arbitrary"),
    )(xr, gr, sc_t, h0, prm['conv_w'], prm['conv_b'], prm['w_rg'], prm['b_rg'], prm['w_ig'], prm['b_ig'], prm['lam'])


def _kth_largest_threshold(stages, shape, n_sel):
    carry = (jnp.full(shape, INT_MIN, jnp.int32), jnp.zeros(shape, F32))
    top = 31
    for count_ge, nbits in stages:

        def bit_step(t, carry, count_ge=count_ge, top=top):
            r, cnt_r = carry
            cand = r + jnp.left_shift(jnp.int32(1), top - t)
            cnt = count_ge(_ordered_to_f32(cand))
            keep = cnt >= float(n_sel)
            return jnp.where(keep, cand, r), jnp.where(keep, cnt, cnt_r)

        carry = lax.fori_loop(0, nbits, bit_step, carry)
        top -= nbits
    r, cnt_r = carry
    return jnp.where(r != INT_MIN, _ordered_to_f32(r), F32_LOWEST), cnt_r


def _kth_largest_via_bf16(count_ge_rounded, count_ge, shape, n_sel):
    k = float(n_sel)

    def grid_f32(p):
        return lax.bitcast_convert_type(jnp.left_shift(jnp.where(p >= 0, p, p ^ jnp.int32(0x7FFF)), 16), F32)

    def coarse_step(t, p):
        cand = p + jnp.left_shift(jnp.int32(1), 15 - t)
        return jnp.where(count_ge_rounded(grid_f32(cand)) >= k, cand, p)

    p_min = -(1 << 15)
    p = lax.fori_loop(0, 16, coarse_step, jnp.full(shape, p_min, jnp.int32))
    found = p != p_min
    centre = jnp.where(p >= 0, jnp.left_shift(p, 16), jnp.left_shift(p, 16) | jnp.int32(0xFFFF))
    lo = jnp.where(found, centre - ((1 << 15) + 1), 0)
    hi = jnp.where(found, centre + ((1 << 16) + 1), 1)

    def unsettled(lo, hi, cnt_lo):
        return jnp.max(jnp.where((hi - lo > 1) & (cnt_lo != k), 1.0, 0.0)) > 0.0

    def cond(state):
        t, _, _, _, go = state
        return jnp.logical_and(t < 18, go)

    def body(state):
        t, lo, hi, cnt_lo, _ = state
        active = hi - lo > 1
        mid = lo + jnp.right_shift(hi - lo, 1)
        cnt = count_ge(_ordered_to_f32(mid))
        up = active & (cnt >= k)
        down = active & (cnt < k)
        lo, cnt_lo, hi = jnp.where(up, mid, lo), jnp.where(up, cnt, cnt_lo), jnp.where(down, mid, hi)
        return t + 1, lo, hi, cnt_lo, unsettled(lo, hi, cnt_lo)

    cnt0 = jnp.full(shape, -1.0, F32)
    _, lo, _, cnt_lo, _ = lax.while_loop(cond, body, (jnp.int32(0), lo, hi, cnt0, unsettled(lo, hi, cnt0)))
    return jnp.where(found, _ordered_to_f32(lo), F32_LOWEST), cnt_lo


def _tie_cut(count_eq_lt, need, nbits, shape):
    def step(t, x):
        cand = x + jnp.left_shift(jnp.int32(1), nbits - 1 - t)
        return jnp.where(count_eq_lt(cand) < need, cand, x)

    return lax.fori_loop(0, nbits, step, jnp.zeros(shape, jnp.int32))


def _keep_f32(s, thr, kpos, cut):
    return jnp.where(s == thr, jnp.where(kpos <= cut, 1.0, 0.0), jnp.where(s > thr, 1.0, 0.0))


def _attn_body(qi_ref, q_ref, wt_ref, kid_ref, kb_ref, vt_ref, qg_ref, kg_ref, ob_ref,
               sc_ref, sb_ref, qis_ref, qs_ref, st_ref, pt_ref, m_ref, l_ref, acc_ref, *, n_sel, s_len):
    tq = q_ref.shape[1]
    kc = tq
    npair = N_HEADS // 2
    nlb = tq // LANES
    i = pl.program_id(1)
    nch = i + 1
    lane = lax.broadcasted_iota(jnp.int32, (1, LANES), 1)
    low_head = lane < HEAD_DIM

    for j in range(npair):
        for src, dst in ((qi_ref, qis_ref), (q_ref, qs_ref)):
            blk = src[0, :, LANES * j:LANES * (j + 1)]
            zero = jnp.zeros_like(blk)
            dst[j, 0:tq, :] = jnp.where(low_head, blk, zero)
            dst[j, tq:2 * tq, :] = jnp.where(low_head, zero, blk)

    qpos = i * tq + lax.broadcasted_iota(jnp.int32, (1, tq), 1)
    kiota = lax.broadcasted_iota(jnp.int32, (kc, 1), 0)
    wt = wt_ref[0]

    npair_chunks = (nch + 1) // 2

    def key_rows(c):
        return pl.ds(pl.multiple_of(jnp.minimum(c, s_len // kc - 1) * kc, kc), kc)

    def index_matmuls(c, slot):
        kic = kid_ref[0, key_rows(c), :]
        for j in range(npair):
            st_ref[slot, j] = lax.dot_general(kic, qis_ref[j], NT_DIMS, preferred_element_type=F32)

    def index_scores(c, slot):
        kpos = c * kc + kiota
        for rb in range(kc // LANES):
            rows = slice(rb * LANES, (rb + 1) * LANES)
            for lb in range(nlb):
                cols = slice(lb * LANES, (lb + 1) * LANES)
                acc = jnp.zeros((LANES, LANES), F32)
                for h in range(IDX_HEADS):
                    j, half = divmod(h, 2)
                    sh = st_ref[slot, j, rows, half * tq + lb * LANES:half * tq + (lb + 1) * LANES]
                    acc = acc + wt[h:h + 1, cols] * jnp.maximum(sh, 0.0)
                val = jnp.where(kpos[rows] <= qpos[:, cols], acc * IDX_SCALE, -jnp.inf)
                sc_ref[c, rows, cols] = val
                sb_ref[c, rows, cols] = val.astype(BF16)

    def index_pair(c2, carry):
        c = 2 * c2
        index_matmuls(c + 1, 1)
        index_scores(c, 0)
        index_matmuls(c + 2, 0)
        index_scores(c + 1, 1)
        return carry

    index_matmuls(0, 0)
    lax.fori_loop(0, npair_chunks, index_pair, 0)

    def count_rows(pred):
        def chunk2(c2, acc):
            for u in range(2):
                c = 2 * c2 + u
                hit = pred(sc_ref[c], c * kc + kiota)
                acc = acc + jnp.sum(hit.reshape(kc // SUBLANES, SUBLANES, tq), axis=0)
            return acc

        acc = lax.fori_loop(0, npair_chunks, chunk2, jnp.zeros((SUBLANES, tq), F32))
        return jnp.sum(acc, axis=0, keepdims=True)

    bf16_rows = 2 * SUBLANES
    one_b = jnp.ones((bf16_rows, tq), BF16)
    zero_b = jnp.zeros((bf16_rows, tq), BF16)

    def count_ge_rounded(cand):
        cb = jnp.broadcast_to(cand, (bf16_rows, tq)).astype(BF16)

        def chunk2(c2, acc):
            for u in range(2):
                sb = sb_ref[2 * c2 + u]
                hits = [jnp.where(sb[bf16_rows * g:bf16_rows * (g + 1)] >= cb, one_b, zero_b)
                        for g in range(kc // bf16_rows)]
                while len(hits) > 1:
                    hits = [a + b for a, b in zip(hits[0::2], hits[1::2])]
                acc = acc + hits[0]
            return acc

        acc = lax.fori_loop(0, npair_chunks, chunk2, zero_b)
        return jnp.sum(acc.astype(F32), axis=0, keepdims=True)

    def count_ge(cand):
        return count_rows(lambda s, kpos: jnp.where(s >= cand, 1.0, 0.0))

    assert (kc // bf16_rows) * s_len // kc <= 256
    thr, cnt = _kth_largest_via_bf16(count_ge_rounded, count_ge, (1, tq), n_sel)
    has_ties = jnp.max(cnt) > float(n_sel)

    bound = 1.02 * HEAD_DIM * Q_SCALE * jnp.max(jnp.abs(qg_ref[...])) * jnp.max(jnp.abs(kg_ref[...]))
    fast = 2.0 * bound <= 120.0
    sel_bias = jnp.where(fast, -bound, 0.0)

    @pl.when(jnp.logical_not(has_ties))
    def _():
        def to_bias(c, carry):
            sc_ref[c] = jnp.where(sc_ref[c] >= thr, sel_bias, NEG)
            return carry

        lax.fori_loop(0, nch, to_bias, 0)

    @pl.when(has_ties)
    def _():
        need = float(n_sel) - count_rows(lambda s, kpos: jnp.where(s > thr, 1.0, 0.0))
        cut = _tie_cut(
            lambda cand: count_rows(lambda s, kpos: jnp.where(s == thr, jnp.where(kpos < cand, 1.0, 0.0), 0.0)),
            need, (s_len - 1).bit_length() + 1, (1, tq))

        def to_bias(c, carry):
            keep = _keep_f32(sc_ref[c], thr, c * kc + kiota, cut)
            sc_ref[c] = jnp.where(keep > 0.5, sel_bias, NEG)
            return carry

        lax.fori_loop(0, nch, to_bias, 0)

    m_ref[...] = jnp.full(m_ref.shape, NEG, F32)
    l_ref[...] = jnp.zeros(l_ref.shape, F32)
    acc_ref[...] = jnp.zeros(acc_ref.shape, F32)

    def logit_matmuls(c, slot):
        for j in range(npair):
            kj = kb_ref[0, key_rows(c), LANES * j:LANES * (j + 1)]
            st_ref[slot, j] = lax.dot_general(kj, qs_ref[j], NT_DIMS, preferred_element_type=F32)

    def weights_and_values(c, slot):
        for h in range(N_HEADS):
            j, half = divmod(h, 2)
            for lb in range(nlb):
                cols = slice(lb * LANES, (lb + 1) * LANES)
                p = jnp.exp2(st_ref[slot, j, :, half * tq + lb * LANES:half * tq + (lb + 1) * LANES]
                             + sc_ref[c, :, cols])
                l_ref[h:h + 1, cols] = l_ref[h:h + 1, cols] + jnp.sum(p, axis=0, keepdims=True)
                pt_ref[h, :, cols] = p.astype(BF16)
        for h in range(N_HEADS):
            hrows = slice(HEAD_DIM * h, HEAD_DIM * (h + 1))
            acc_ref[hrows, :] = acc_ref[hrows, :] + jnp.dot(vt_ref[0, c, hrows, :], pt_ref[h],
                                                            preferred_element_type=F32)

    def attend_shifted(c, carry):
        logit_matmuls(c, 0)
        weights_and_values(c, 0)
        return carry

    def attend_online(c, carry):
        logit_matmuls(c, 0)
        for h in range(N_HEADS):
            j, half = divmod(h, 2)
            alphas = []
            for lb in range(nlb):
                cols = slice(lb * LANES, (lb + 1) * LANES)
                s = st_ref[0, j, :, half * tq + lb * LANES:half * tq + (lb + 1) * LANES] + sc_ref[c, :, cols]
                m_old = m_ref[h:h + 1, cols]
                m_new = jnp.maximum(m_old, jnp.max(s, axis=0, keepdims=True))
                alpha = jnp.exp2(m_old - m_new)
                p = jnp.exp2(s - m_new)
                l_ref[h:h + 1, cols] = alpha * l_ref[h:h + 1, cols] + jnp.sum(p, axis=0, keepdims=True)
                m_ref[h:h + 1, cols] = m_new
                pt_ref[h, :, cols] = p.astype(BF16)
                alphas.append(alpha)
            hrows = slice(HEAD_DIM * h, HEAD_DIM * (h + 1))
            pv = jnp.dot(vt_ref[0, c, hrows, :], pt_ref[h], preferred_element_type=F32)
            acc_ref[hrows, :] = jnp.concatenate(alphas, axis=1) * acc_ref[hrows, :] + pv
        return carry

    @pl.when(fast)
    def _():
        lax.fori_loop(0, nch, attend_shifted, 0)

    @pl.when(jnp.logical_not(fast))
    def _():
        lax.fori_loop(0, nch, attend_online, 0)

    linv = 1.0 / l_ref[...]
    for h in range(N_HEADS):
        hrows = slice(HEAD_DIM * h, HEAD_DIM * (h + 1))
        acc_ref[hrows, :] = acc_ref[hrows, :] * linv[h:h + 1]
    ob_ref[0] = acc_ref[...].T.astype(BF16)


def _attn_prompt(qi, q, wt, kid, kb, vt, qg, kg, n_sel, tq):
    b, s, _ = q.shape
    nq = s // tq
    blk = lambda i, j: (i, j, 0)
    per_b = lambda i, j: (i, 0, 0)
    gain = pl.BlockSpec((1, D_ATT), lambda i, j: (0, 0))
    return pl.pallas_call(
        functools.partial(_attn_body, n_sel=n_sel, s_len=s),
        name="attn_prompt",
        grid=(b, nq),
        in_specs=[pl.BlockSpec((1, tq, D_ATT), blk), pl.BlockSpec((1, tq, D_ATT), blk),
                  pl.BlockSpec((1, IDX_HEADS, tq), lambda i, j: (i, 0, j)), pl.BlockSpec((1, s, LANES), per_b),
                  pl.BlockSpec((1, s, D_ATT), per_b), pl.BlockSpec((1, nq, D_ATT, tq), lambda i, j: (i, 0, 0, 0)),
                  gain, gain],
        out_specs=pl.BlockSpec((1, tq, D_ATT), blk),
        out_shape=jax.ShapeDtypeStruct((b, s, D_ATT), BF16),
        scratch_shapes=[pltpu.VMEM((nq + nq % 2, tq, tq), F32),
                        pltpu.VMEM((nq + nq % 2, tq, tq), BF16),
                        pltpu.VMEM((N_HEADS // 2, 2 * tq, LANES), BF16),
                        pltpu.VMEM((N_HEADS // 2, 2 * tq, LANES), BF16),
                        pltpu.VMEM((2, N_HEADS // 2, tq, 2 * tq), F32),
                        pltpu.VMEM((N_HEADS, tq, tq), BF16),
                        pltpu.VMEM((N_HEADS, tq), F32),
                        pltpu.VMEM((N_HEADS, tq), F32),
                        pltpu.VMEM((D_ATT, tq), F32)],
        compiler_params=_cparams("parallel", "arbitrary"),
    )(qi, q, wt, kid, kb, vt, qg, kg)


def _sample_index_body(pt_ref, qh_ref, wcol_ref, *rest):
    del pt_ref
    *page_refs, out_ref = rest
    kt = jnp.concatenate([r[0] for r in page_refs], axis=1).astype(BF16)
    s = jnp.dot(qh_ref[0], kt, preferred_element_type=F32)
    t = jnp.maximum(s, 0.0) * wcol_ref[0]
    out_ref[0] = jnp.sum(t, axis=0, keepdims=True) * IDX_SCALE


def _sample_index(page_table, qh, wcol, cki_t, pp):
    bd, npg = page_table.shape
    per_b = lambda b, c, pt: (b, 0, 0)
    page_spec = lambda j: pl.BlockSpec((1, IDX_DIM, PAGE_SIZE), lambda b, c, pt: (pt[b, c * pp + j], 0, 0))
    grid_spec = pltpu.PrefetchScalarGridSpec(
        num_scalar_prefetch=1,
        grid=(bd, npg // pp),
        in_specs=[pl.BlockSpec((1, IDX_HEADS, IDX_DIM), per_b), pl.BlockSpec((1, IDX_HEADS, 1), per_b)]
                 + [page_spec(j) for j in range(pp)],
        out_specs=pl.BlockSpec((1, 1, pp * PAGE_SIZE), lambda b, c, pt: (b, 0, c)),
    )
    return pl.pallas_call(
        _sample_index_body,
        name="sample_index",
        grid_spec=grid_spec,
        out_shape=jax.ShapeDtypeStruct((bd, 1, npg * PAGE_SIZE), F32),
        compiler_params=_cparams("parallel", "arbitrary"),
    )(page_table, qh, wcol, *([cki_t] * pp))


def _sample_select_body(sc_ref, qi_ref, kid_ref, zkw_ref, bdh_ref, bias_ref, biasn_ref, *, n_sel, chunk):
    rows, past = sc_ref.shape
    lane = lax.broadcasted_iota(jnp.int32, (1, LANES), 1)
    kid = kid_ref[...].astype(F32)
    prod = qi_ref[...].astype(F32) * jnp.concatenate([kid] * (IDX_HEADS * IDX_DIM // LANES), axis=1)
    sh = jnp.zeros((rows, LANES), F32)
    for part in _split3(prod):
        sh = sh + jnp.dot(part, bdh_ref[...], preferred_element_type=F32)
    wi = pltpu.roll(zkw_ref[...], LANES - IDX_DIM, 1)
    new = jnp.sum(jnp.where(lane < IDX_HEADS, jnp.maximum(sh, 0.0) * wi, 0.0), axis=-1, keepdims=True) * IDX_SCALE

    nchunk = past // chunk
    ciota = lax.broadcasted_iota(jnp.int32, (1, chunk), 1)

    def count_rows(pred):
        def cnt_chunk(c, acc):
            off = pl.multiple_of(c * chunk, chunk)
            hit = pred(sc_ref[:, pl.ds(off, chunk)], off + ciota)
            for g in range(chunk // LANES):
                acc = acc + hit[:, LANES * g:LANES * (g + 1)]
            return acc

        acc = lax.fori_loop(0, nchunk, cnt_chunk, jnp.zeros((rows, LANES), F32))
        return jnp.sum(acc, axis=-1, keepdims=True)

    thr, _ = _kth_largest_threshold(
        [(lambda cand: count_rows(lambda s, kpos: jnp.where(s >= cand, 1.0, 0.0)) + jnp.where(new >= cand, 1.0, 0.0),
          32)], (rows, 1), n_sel)
    need = float(n_sel) - (count_rows(lambda s, kpos: jnp.where(s > thr, 1.0, 0.0)) + jnp.where(new > thr, 1.0, 0.0))
    cut = _tie_cut(
        lambda cand: count_rows(lambda s, kpos: jnp.where(s == thr, jnp.where(kpos < cand, 1.0, 0.0), 0.0))
        + jnp.where(new == thr, jnp.where(past < cand, 1.0, 0.0), 0.0),
        need, past.bit_length() + 1, (rows, 1))

    def emit(c, carry):
        off = pl.multiple_of(c * chunk, chunk)
        keep = _keep_f32(sc_ref[:, pl.ds(off, chunk)], thr, off + ciota, cut)
        bias_ref[:, pl.ds(off, chunk)] = jnp.where(keep > 0.5, 0.0, NEG)
        return carry

    lax.fori_loop(0, nchunk, emit, 0)
    keep_new = _keep_f32(new, thr, jnp.int32(past), cut)
    biasn_ref[...] = jnp.broadcast_to(jnp.where(keep_new > 0.5, 0.0, NEG), (rows, LANES))


def _sample_select(sc_nat, qi, kid, zkw, bdh, n_sel):
    bd, past = sc_nat.shape
    rows = min(bd, 128)
    chunk = min(past, 512)
    rb = lambda i: (i, 0)
    return pl.pallas_call(
        functools.partial(_sample_select_body, n_sel=n_sel, chunk=chunk),
        name="sample_select",
        grid=(bd // rows,),
        in_specs=[pl.BlockSpec((rows, past), rb), pl.BlockSpec((rows, IDX_HEADS * IDX_DIM), rb),
                  pl.BlockSpec((rows, LANES), rb), pl.BlockSpec((rows, LANES), rb),
                  pl.BlockSpec((IDX_HEADS * IDX_DIM, LANES), lambda i: (0, 0))],
        out_specs=[pl.BlockSpec((rows, past), rb), pl.BlockSpec((rows, LANES), rb)],
        out_shape=[jax.ShapeDtypeStruct((bd, past), F32), jax.ShapeDtypeStruct((bd, LANES), F32)],
        compiler_params=_cparams("parallel"),
    )(sc_nat, qi, kid, zkw, bdh)


def _sample_attend_body(pt_ref, q_ref, bias_ref, biasn_ref, kn_ref, vn_ref, *rest, pp):
    del pt_ref
    k_refs = rest[:pp]
    v_refs = rest[pp:2 * pp]
    ob_ref, m_ref, l_ref, acc_ref = rest[2 * pp:]
    c = pl.program_id(1)

    @pl.when(c == 0)
    def _():
        m_ref[...] = jnp.full(m_ref.shape, NEG, F32)
        l_ref[...] = jnp.zeros(l_ref.shape, F32)
        acc_ref[...] = jnp.zeros(acc_ref.shape, F32)

    sub = lax.broadcasted_iota(jnp.int32, (N_HEADS, D_ATT), 0)
    lane = lax.broadcasted_iota(jnp.int32, (N_HEADS, D_ATT), 1)
    own = lane // HEAD_DIM == sub
    qrow = jnp.broadcast_to(q_ref[0].astype(F32), (N_HEADS, D_ATT))
    qbd = jnp.where(own, qrow, 0.0)

    def pages_t(refs):
        return jnp.concatenate([r[0].reshape(D_ATT, PAGE_SIZE) for r in refs], axis=1).astype(BF16)

    s = jnp.dot(qbd.astype(BF16), pages_t(k_refs), preferred_element_type=F32) + bias_ref[0]
    m_old = m_ref[...]
    m_new = jnp.maximum(m_old, jnp.max(s, axis=-1, keepdims=True))
    alpha = jnp.exp2(m_old - m_new)
    p = jnp.exp2(s - m_new)
    l_ref[...] = alpha * l_ref[...] + jnp.sum(p, axis=-1, keepdims=True)
    m_ref[...] = m_new
    pv = lax.dot_general(p.astype(BF16), pages_t(v_refs), NT_DIMS, preferred_element_type=F32)
    acc_ref[...] = alpha * acc_ref[...] + pv

    @pl.when(c == pl.num_programs(1) - 1)
    def _():
        s_new = jnp.sum(qbd * kn_ref[0].astype(F32), axis=-1, keepdims=True) + biasn_ref[0][:, 0:1]
        m_old = m_ref[...]
        m_new = jnp.maximum(m_old, s_new)
        alpha = jnp.exp2(m_old - m_new)
        p_new = jnp.exp2(s_new - m_new)
        l = alpha * l_ref[...] + p_new
        acc = alpha * acc_ref[...] + p_new.astype(BF16).astype(F32) * vn_ref[0].astype(F32)
        ob_ref[0] = jnp.sum(jnp.where(own, acc / l, 0.0), axis=0, keepdims=True).astype(BF16)


def _sample_attend(page_table, q, bias, biasn, kn, vn, ck_t, cv_t, pp):
    bd, npg = page_table.shape
    per_b = lambda b, c, pt: (b, 0, 0)
    row_spec = pl.BlockSpec((1, 1, D_ATT), per_b)
    page_spec = lambda j: pl.BlockSpec((1, N_HEADS, HEAD_DIM, PAGE_SIZE),
                                       lambda b, c, pt: (pt[b, c * pp + j], 0, 0, 0))
    grid_spec = pltpu.PrefetchScalarGridSpec(
        num_scalar_prefetch=1,
        grid=(bd, npg // pp),
        in_specs=[row_spec, pl.BlockSpec((1, 1, pp * PAGE_SIZE), lambda b, c, pt: (b, 0, c)),
                  pl.BlockSpec((1, 1, LANES), per_b), row_spec, row_spec] + [page_spec(j) for j in range(pp)] * 2,
        out_specs=row_spec,
        scratch_shapes=[pltpu.VMEM((N_HEADS, 1), F32), pltpu.VMEM((N_HEADS, 1), F32),
                        pltpu.VMEM((N_HEADS, D_ATT), F32)],
    )
    return pl.pallas_call(
        functools.partial(_sample_attend_body, pp=pp),
        name="sample_attend",
        grid_spec=grid_spec,
        out_shape=jax.ShapeDtypeStruct((bd, 1, D_ATT), BF16),
        compiler_params=_cparams("parallel", "arbitrary"),
    )(page_table, q, bias, biasn, kn, vn, *([ck_t] * pp), *([cv_t] * pp))


def _merge_body(x_ref, oa_ref, ob_ref, ga_ref, gb_ref, wa_ref, wb_ref, wo_ref, g2_ref, wfg_ref, wfu_ref, wfd_ref,
                y_ref, *, ff_chunk):
    ma = jnp.dot(oa_ref[...], wa_ref[...], preferred_element_type=F32)
    mb = jnp.dot(ob_ref[...], wb_ref[...], preferred_element_type=F32)
    m = _sigmoid(ga_ref[...]) * ma + _sigmoid(gb_ref[...]) * mb
    x1 = x_ref[...] + jnp.dot(m.astype(BF16), wo_ref[...], preferred_element_type=F32)
    ms = jnp.mean(x1 * x1, axis=-1, keepdims=True)
    h = (x1 * lax.rsqrt(ms + EPS) * g2_ref[...]).astype(BF16)
    d_ff = wfg_ref.shape[1]
    y = x1
    for lo in range(0, d_ff, ff_chunk):
        g = jnp.dot(h, wfg_ref[:, lo:lo + ff_chunk], preferred_element_type=F32)
        u = jnp.dot(h, wfu_ref[:, lo:lo + ff_chunk], preferred_element_type=F32)
        act = (g * _sigmoid(g) * u).astype(BF16)
        y = y + jnp.dot(act, wfd_ref[lo:lo + ff_chunk, :], preferred_element_type=F32)
    y_ref[...] = y


def _merge(x2d, oa, ob, ga, gb, prm, tm):
    n = x2d.shape[0]
    row = lambda i: (i, 0)
    d_ff = prm['w_fg'].shape[1]
    ff_chunk = 256 if d_ff % 256 == 0 else d_ff
    return pl.pallas_call(
        functools.partial(_merge_body, ff_chunk=ff_chunk),
        name="merge_ffn",
        grid=(n // tm,),
        in_specs=[pl.BlockSpec((tm, D_MODEL), row), pl.BlockSpec((tm, D_RNN), row), pl.BlockSpec((tm, D_ATT), row),
                  pl.BlockSpec((tm, D_MODEL), row), pl.BlockSpec((tm, D_MODEL), row),
                  _const_spec(prm['w_a'].shape), _const_spec(prm['w_b'].shape), _const_spec(prm['w_o'].shape),
                  _const_spec((1, D_MODEL)), _const_spec(prm['w_fg'].shape), _const_spec(prm['w_fu'].shape),
                  _const_spec(prm['w_fd'].shape)],
        out_specs=pl.BlockSpec((tm, D_MODEL), row),
        out_shape=jax.ShapeDtypeStruct((n, D_MODEL), F32),
        compiler_params=_cparams("parallel"),
    )(x2d, oa, ob, ga, gb, prm['w_a'], prm['w_b'], prm['w_o'], prm['g2'], prm['w_fg'], prm['w_fu'], prm['w_fd'])


def _rope_tables(pos):
    half = HEAD_DIM // 2
    inv = 1.0 / (ROPE_THETA ** (jnp.arange(half, dtype=F32) / half))
    ang = pos.astype(F32)[:, None] * inv[None, :]
    cos, sin = jnp.cos(ang), jnp.sin(ang)
    cos_h = jnp.concatenate([cos, cos], axis=-1)
    sin_h = jnp.concatenate([-sin, sin], axis=-1)
    return jnp.tile(cos_h, (1, LANES // HEAD_DIM)), jnp.tile(sin_h, (1, LANES // HEAD_DIM))


def _block_diag(w):
    nb, bw, _ = w.shape
    eye = jnp.eye(nb, dtype=w.dtype)
    return (eye[:, None, :, None] * w[:, :, None, :]).reshape(nb * bw, nb * bw)


def _layer_params(l, norm1_g, w_in, conv_w, conv_b, w_rg, b_rg, w_ig, b_ig, lru_lambda, q_norm_g, k_norm_g,
                  k_idx_norm_g, w_branch_a, w_branch_b, w_out, norm2_g, w_ffn_gate, w_ffn_up, w_ffn_down):
    w = w_in[l]
    o_ki = 2 * D_RNN + 3 * D_ATT + IDX_HEADS * IDX_DIM
    o_g = o_ki + IDX_DIM + IDX_HEADS
    w_kw = jnp.pad(w[:, o_ki:o_g], ((0, 0), (0, LANES - IDX_DIM - IDX_HEADS)))
    head_of = np.arange(D_ATT) // HEAD_DIM
    bd_head = jnp.asarray((head_of[:, None] == head_of[None, :]) / HEAD_DIM, BF16)
    bd_sum = jnp.asarray(head_of[:, None] == np.arange(LANES)[None, :], BF16)
    tile_h = lambda g: jnp.tile(g[l], N_HEADS)[None, :]
    return dict(
        g1=norm1_g[l][None, :], w_main=w[:, :o_ki].astype(BF16), w_kw=w_kw.astype(BF16),
        w_gate=w[:, o_g:].astype(BF16), qg=tile_h(q_norm_g), kg=tile_h(k_norm_g),
        kig=jnp.pad(k_idx_norm_g[l], (0, LANES - IDX_DIM))[None, :], bd_head=bd_head, bd_sum=bd_sum,
        conv_w=conv_w[l], conv_b=conv_b[l][None, :], w_rg=_block_diag(w_rg[l]).astype(BF16), b_rg=b_rg[l][None, :],
        w_ig=_block_diag(w_ig[l]).astype(BF16), b_ig=b_ig[l][None, :], lam=lru_lambda[l][None, :],
        w_a=w_branch_a[l].astype(BF16), w_b=w_branch_b[l].astype(BF16), w_o=w_out[l].astype(BF16),
        g2=norm2_g[l][None, :], w_fg=w_ffn_gate[l].astype(BF16), w_fu=w_ffn_up[l].astype(BF16),
        w_fd=w_ffn_down[l].astype(BF16))


def _pick_tile(n, pref):
    t = min(n, pref)
    while n % t:
        t //= 2
    return t


def _prompt_layer(x, prm):
    b, s, _ = x.shape
    n_sel = min(TOPK_MAX, s // 4)
    tm = tq = _pick_tile(s, 256)
    cos, sin = _rope_tables(jnp.arange(s))
    xr, gr, q, kt, kb, vt, vb, vtb, qi, kit, kid, zkw, ga, gb = _proj(x.reshape(b * s, D_MODEL), cos, sin, prm, tm, s)
    r3 = lambda a: a.reshape(b, s, a.shape[-1])
    xr3 = r3(xr)
    oa, h_last = _rglru_seq(xr3, r3(gr), jnp.zeros((b, SUBLANES, D_RNN), F32), jnp.zeros((b, 1, D_RNN), F32), prm,
                            _pick_tile(s, 256))
    wt = r3(zkw)[:, :, IDX_DIM:IDX_DIM + IDX_HEADS].transpose(0, 2, 1)
    ob = _attn_prompt(r3(qi), r3(q), wt, r3(kid), r3(kb), vtb, prm['qg'], prm['kg'], n_sel, tq)
    y = _merge(x.reshape(b * s, D_MODEL), oa.reshape(b * s, D_RNN), ob.reshape(b * s, D_ATT), ga, gb, prm,
               _pick_tile(s, 512))
    xpad = jnp.concatenate([jnp.zeros((b, CONV_W - 1, D_RNN), F32), xr3], axis=1)
    heads_last = lambda a: a.reshape(b, N_HEADS, HEAD_DIM, s).transpose(0, 3, 1, 2)
    return (y.reshape(b, s, D_MODEL), heads_last(kt), heads_last(vt), kit.transpose(0, 2, 1),
            xpad[:, -(CONV_W - 1):], h_last.reshape(b, D_RNN))


def _sample_layer(x, cache_k, cache_v, cache_k_idx, state_conv, state_h, page_table, prm):
    bd, t, _ = x.shape
    assert t == 1
    npg = page_table.shape[1]
    past = npg * PAGE_SIZE
    n_sel = min(TOPK_MAX, (past + t) // 4)
    cos, sin = _rope_tables(jnp.full((bd,), past))
    xr, gr, q, kt, kb, vt, vb, _, qi, kit, kid, zkw, ga, gb = _proj(x.reshape(bd, D_MODEL), cos, sin, prm, bd, bd)
    oa, h_new = _rglru_step(xr, gr, state_conv.transpose(1, 0, 2), state_h, prm)

    qh = qi.reshape(bd, IDX_HEADS, IDX_DIM)
    wcol = zkw[:, IDX_DIM:IDX_DIM + IDX_HEADS, None]
    sc = _sample_index(page_table, qh, wcol, cache_k_idx.transpose(0, 2, 1), _pick_tile(npg, 32))
    bias, biasn = _sample_select(sc.reshape(bd, past), qi, kid, zkw, prm['bd_sum'], n_sel)
    ob = _sample_attend(page_table, q[:, None, :], bias[:, None, :], biasn[:, None, :], kb[:, None, :],
                        vb[:, None, :], cache_k.transpose(0, 2, 3, 1), cache_v.transpose(0, 2, 3, 1),
                        _pick_tile(npg, 16))
    y = _merge(x.reshape(bd, D_MODEL), oa, ob.reshape(bd, D_ATT), ga, gb, prm, bd)
    conv_new = jnp.concatenate([state_conv, xr[:, None, :]], axis=1)[:, -(CONV_W - 1):]
    heads_last = lambda a: a.reshape(N_HEADS, HEAD_DIM, bd).transpose(2, 0, 1)[:, None]
    return (y.reshape(bd, 1, D_MODEL), heads_last(kt), heads_last(vt), kit[0].T[:, None, :], conv_new, h_new)


def kernel(x_prompt, x_sample, cache_k, cache_v, cache_k_idx, state_conv, state_h, page_table, norm1_g, w_in, conv_w, conv_b, w_rg, b_rg, w_ig, b_ig, lru_lambda, q_norm_g, k_norm_g, k_idx_norm_g, w_branch_a, w_branch_b, w_out, norm2_g, w_ffn_gate, w_ffn_up, w_ffn_down):
    depth = w_in.shape[0]
    yp, ys = x_prompt, x_sample
    outs_p, outs_s = [], []
    for l in range(depth):
        prm = _layer_params(l, norm1_g, w_in, conv_w, conv_b, w_rg, b_rg, w_ig, b_ig, lru_lambda, q_norm_g,
                            k_norm_g, k_idx_norm_g, w_branch_a, w_branch_b, w_out, norm2_g, w_ffn_gate, w_ffn_up,
                            w_ffn_down)
        yp, *rest_p = _prompt_layer(yp, prm)
        ys, *rest_s = _sample_layer(ys, cache_k[l], cache_v[l], cache_k_idx[l], state_conv[l], state_h[l],
                                    page_table, prm)
        outs_p.append(rest_p)
        outs_s.append(rest_s)
    stack = lambda outs, i: jnp.stack([o[i] for o in outs])
    return (yp, ys, *[stack(outs_p, i) for i in range(5)], *[stack(outs_s, i) for i in range(5)])
```

```python
import functools

import jax
import jax.numpy as jnp
import numpy as np
from jax import lax
from jax.experimental import pallas as pl
from jax.experimental.pallas import tpu as pltpu

F32 = jnp.float32
BF16 = jnp.bfloat16

D_MODEL = 1024
D_RNN = 512
RNN_BLOCKS = 8
CONV_W = 4
LRU_C = 8.0
N_HEADS = 8
HEAD_DIM = 64
D_ATT = N_HEADS * HEAD_DIM
IDX_HEADS = 8
IDX_DIM = 64
TOPK_MAX = 256
PAGE_SIZE = 128
ROPE_THETA = 10000.0
EPS = 1e-6
LANES = 128
SUBLANES = 8
IDX_SCALE = IDX_HEADS ** -0.5 * IDX_DIM ** -0.5
Q_SCALE = HEAD_DIM ** -0.5 * float(np.log2(np.e))
NEG = -1e30
INT_MIN = -(2 ** 31)
F32_LOWEST = float(np.finfo(np.float32).min)
VMEM_LIMIT = 56 * 1024 * 1024

NT_DIMS = (((1,), (1,)), ((), ()))


def _cparams(*sem):
    return pltpu.CompilerParams(dimension_semantics=sem, vmem_limit_bytes=VMEM_LIMIT)


def _const_spec(shape):
    nd = len(shape)
    return pl.BlockSpec(shape, lambda *_: (0,) * nd, pipeline_mode=pl.Buffered(1))


def _split3(x):
    a = x.astype(BF16)
    r = x - a.astype(F32)
    b = r.astype(BF16)
    c = (r - b.astype(F32)).astype(BF16)
    return a, b, c


def _sigmoid(x):
    return 1.0 / (1.0 + jnp.exp(-x))


def _gelu_tanh(x):
    return 0.5 * x * (1.0 + jnp.tanh(np.sqrt(2.0 / np.pi) * (x + 0.044715 * (x * x * x))))


def _ordered_to_f32(o):
    bits = jnp.where(o >= 0, o, o ^ jnp.int32(0x7FFFFFFF))
    return lax.bitcast_convert_type(bits, F32)


def _proj_body(x_ref, g1_ref, wm_ref, wkw_ref, wg_ref, cs_ref, sn_ref, qg_ref, kg_ref, kig_ref, bd_ref,
               xr_ref, gr_ref, q_ref, kt_ref, kb_ref, vt_ref, vb_ref, vtb_ref, qi_ref, kit_ref, kid_ref, zkw_ref,
               ga_ref, gb_ref):
    x = x_ref[...]
    ms = jnp.mean(x * x, axis=-1, keepdims=True)
    h = (x * lax.rsqrt(ms + EPS) * g1_ref[...]).astype(BF16)

    def mm(lo, hi):
        return jnp.dot(h, wm_ref[:, lo:hi], preferred_element_type=F32)

    cs = cs_ref[...]
    sn = sn_ref[...]
    lane = lax.broadcasted_iota(jnp.int32, (1, LANES), 1)
    first_half = (lane % HEAD_DIM) < (HEAD_DIM // 2)
    low_head = lane < HEAD_DIM

    def rope128(xs):
        sw = jnp.where(first_half, pltpu.roll(xs, LANES - HEAD_DIM // 2, 1), pltpu.roll(xs, HEAD_DIM // 2, 1))
        return xs * cs + sw * sn

    def rope(xn):
        return jnp.concatenate([rope128(xn[:, LANES * j:LANES * (j + 1)]) for j in range(D_ATT // LANES)], axis=1)

    bd = bd_ref[...]

    def headnorm(z, g):
        a, b, c = _split3(z * z)
        msq = (jnp.dot(a, bd, preferred_element_type=F32) + jnp.dot(b, bd, preferred_element_type=F32)
               + jnp.dot(c, bd, preferred_element_type=F32))
        return z * lax.rsqrt(msq + EPS) * g

    xr_ref[...] = mm(0, 512)
    gr_ref[...] = mm(512, 1024)
    q = rope(headnorm(mm(1024, 1536), qg_ref[...]))
    q_ref[...] = (q * Q_SCALE).astype(BF16)
    k = rope(headnorm(mm(1536, 2048), kg_ref[...]))
    kb_ref[...] = k.astype(BF16)
    v = mm(2048, 2560)
    vb_ref[...] = v.astype(BF16)
    kt_ref[0] = k.T
    vt = v.T
    vt_ref[0] = vt
    vtb_ref[0, 0] = vt.astype(BF16)
    qi_ref[...] = rope(mm(2560, 3072)).astype(BF16)

    zkw = jnp.dot(h, wkw_ref[...], preferred_element_type=F32)
    zkw_ref[...] = zkw
    kms = jnp.sum(jnp.where(low_head, zkw * zkw, 0.0), axis=-1, keepdims=True) * (1.0 / IDX_DIM)
    kir = rope128(zkw * lax.rsqrt(kms + EPS) * kig_ref[...])
    kit_ref[0] = kir.T[:IDX_DIM]
    kid_ref[...] = jnp.where(low_head, kir, pltpu.roll(kir, HEAD_DIM, 1)).astype(BF16)

    ga_ref[...] = jnp.dot(h, wg_ref[:, :D_MODEL], preferred_element_type=F32)
    gb_ref[...] = jnp.dot(h, wg_ref[:, D_MODEL:], preferred_element_type=F32)


def _proj(x2d, tab_cos, tab_sin, prm, tm, seq):
    n = x2d.shape[0]
    nblk = seq // tm
    nseq = n // seq
    row = lambda i: (i, 0)
    tab = lambda i: (i % nblk, 0)
    rows_of = lambda w, dt: (pl.BlockSpec((tm, w), row), jax.ShapeDtypeStruct((n, w), dt))
    cols_of = lambda w, dt: (pl.BlockSpec((1, w, tm), lambda i: (i // nblk, 0, i % nblk)),
                             jax.ShapeDtypeStruct((nseq, w, seq), dt))
    vtb = (pl.BlockSpec((1, 1, D_ATT, tm), lambda i: (i // nblk, i % nblk, 0, 0)),
           jax.ShapeDtypeStruct((nseq, nblk, D_ATT, tm), BF16))
    outs = [rows_of(D_RNN, F32), rows_of(D_RNN, F32), rows_of(D_ATT, BF16), cols_of(D_ATT, F32), rows_of(D_ATT, BF16),
            cols_of(D_ATT, F32), rows_of(D_ATT, BF16), vtb, rows_of(IDX_HEADS * IDX_DIM, BF16), cols_of(IDX_DIM, F32),
            rows_of(LANES, BF16), rows_of(LANES, F32), rows_of(D_MODEL, F32), rows_of(D_MODEL, F32)]
    return pl.pallas_call(
        _proj_body,
        name="proj",
        grid=(n // tm,),
        in_specs=[pl.BlockSpec((tm, D_MODEL), row), _const_spec((1, D_MODEL)),
                  _const_spec(prm['w_main'].shape), _const_spec(prm['w_kw'].shape), _const_spec(prm['w_gate'].shape),
                  pl.BlockSpec((tm, LANES), tab), pl.BlockSpec((tm, LANES), tab),
                  _const_spec((1, D_ATT)), _const_spec((1, D_ATT)), _const_spec((1, LANES)),
                  _const_spec((D_ATT, D_ATT))],
        out_specs=[spec for spec, _ in outs],
        out_shape=[shape for _, shape in outs],
        compiler_params=_cparams("parallel"),
    )(x2d, prm['g1'], prm['w_main'], prm['w_kw'], prm['w_gate'], tab_cos, tab_sin,
      prm['qg'], prm['kg'], prm['kig'], prm['bd_head'])


def _lru_coeffs(xc, wrg_ref, brg_ref, wig_ref, big_ref, lam_ref):
    xcb = xc.astype(BF16)
    r = _sigmoid(jnp.dot(xcb, wrg_ref[...], preferred_element_type=F32) + brg_ref[...])
    i = _sigmoid(jnp.dot(xcb, wig_ref[...], preferred_element_type=F32) + big_ref[...])
    nl = -lam_ref[...]
    softplus = jnp.maximum(nl, 0.0) + jnp.log1p(jnp.exp(-jnp.abs(nl)))
    log_a = -LRU_C * r * softplus
    a = jnp.exp(log_a)
    b = jnp.sqrt(-jnp.tanh(log_a) * (a * a + 1.0)) * i * xc
    return a, b


def _rglru_seq_body(xr_ref, gr_ref, cw_ref, cb_ref, wrg_ref, brg_ref, wig_ref, big_ref, lam_ref, cbuf_ref, h0_ref,
                    oa_ref, hl_ref, prev_ref, h_ref, a_s, b_s):
    tc = xr_ref.shape[1]
    ng = tc // SUBLANES

    @pl.when(pl.program_id(1) == 0)
    def _():
        prev_ref[...] = cbuf_ref[0]
        h_ref[...] = jnp.broadcast_to(h0_ref[0], (SUBLANES, D_RNN))

    x = xr_ref[0]
    ext = jnp.concatenate([prev_ref[...], x], axis=0)
    cw = cw_ref[...]
    xc = cb_ref[...] + cw[0:1] * ext[SUBLANES - 3:SUBLANES - 3 + tc]
    xc = xc + cw[1:2] * ext[SUBLANES - 2:SUBLANES - 2 + tc]
    xc = xc + cw[2:3] * ext[SUBLANES - 1:SUBLANES - 1 + tc]
    xc = xc + cw[3:4] * x
    prev_ref[...] = x[tc - SUBLANES:tc]

    a, b = _lru_coeffs(xc, wrg_ref, brg_ref, wig_ref, big_ref, lam_ref)

    row = lax.broadcasted_iota(jnp.int32, (tc, 1), 0) % SUBLANES
    d = 1
    while d < SUBLANES:
        keep = row >= d
        a_sh = jnp.where(keep, pltpu.roll(a, d, 0), 1.0)
        b_sh = jnp.where(keep, pltpu.roll(b, d, 0), 0.0)
        b = a * b_sh + b
        a = a * a_sh
        d *= 2
    a_s[...] = a
    b_s[...] = b

    def step(g, h):
        off = pl.multiple_of(g * SUBLANES, SUBLANES)
        hr = a_s[pl.ds(off, SUBLANES), :] * h + b_s[pl.ds(off, SUBLANES), :]
        b_s[pl.ds(off, SUBLANES), :] = hr
        return jnp.broadcast_to(hr[SUBLANES - 1:SUBLANES], (SUBLANES, D_RNN))

    h = lax.fori_loop(0, ng, step, h_ref[...])
    h_ref[...] = h
    oa_ref[0] = (b_s[...] * _gelu_tanh(gr_ref[0])).astype(BF16)
    hl_ref[0] = h[0:1]


def _rglru_seq(xr, gr, cbuf8, h0, prm, tc):
    b, s, _ = xr.shape
    seq = lambda i, c: (i, c, 0)
    per_b = lambda i, c: (i, 0, 0)
    vec = _const_spec((1, D_RNN))
    return pl.pallas_call(
        _rglru_seq_body,
        name="rglru_seq",
        grid=(b, s // tc),
        in_specs=[pl.BlockSpec((1, tc, D_RNN), seq), pl.BlockSpec((1, tc, D_RNN), seq),
                  _const_spec((CONV_W, D_RNN)), vec, _const_spec((D_RNN, D_RNN)), vec,
                  _const_spec((D_RNN, D_RNN)), vec, vec,
                  pl.BlockSpec((1, SUBLANES, D_RNN), per_b), pl.BlockSpec((1, 1, D_RNN), per_b)],
        out_specs=[pl.BlockSpec((1, tc, D_RNN), seq), pl.BlockSpec((1, 1, D_RNN), per_b)],
        out_shape=[jax.ShapeDtypeStruct((b, s, D_RNN), BF16), jax.ShapeDtypeStruct((b, 1, D_RNN), F32)],
        scratch_shapes=[pltpu.VMEM((SUBLANES, D_RNN), F32), pltpu.VMEM((SUBLANES, D_RNN), F32),
                        pltpu.VMEM((tc, D_RNN), F32), pltpu.VMEM((tc, D_RNN), F32)],
        compiler_params=_cparams("parallel", "arbitrary"),
    )(xr, gr, prm['conv_w'], prm['conv_b'], prm['w_rg'], prm['b_rg'], prm['w_ig'], prm['b_ig'], prm['lam'],
      cbuf8, h0)


def _rglru_step_body(xr_ref, gr_ref, sc_ref, h0_ref, cw_ref, cb_ref, wrg_ref, brg_ref, wig_ref, big_ref, lam_ref,
                     oa_ref, h_ref):
    cw = cw_ref[...]
    x = xr_ref[...]
    xc = cb_ref[...] + cw[0:1] * sc_ref[0]
    xc = xc + cw[1:2] * sc_ref[1]
    xc = xc + cw[2:3] * sc_ref[2]
    xc = xc + cw[3:4] * x
    a, b = _lru_coeffs(xc, wrg_ref, brg_ref, wig_ref, big_ref, lam_ref)
    h = a * h0_ref[...] + b
    h_ref[...] = h
    oa_ref[...] = (h * _gelu_tanh(gr_ref[...])).astype(BF16)


def _rglru_step(xr, gr, sc_t, h0, prm):
    n = xr.shape[0]
    full = lambda shape: pl.BlockSpec(shape, lambda i: (0,) * len(shape))
    vec = full((1, D_RNN))
    return pl.pallas_call(
        _rglru_step_body,
        name="rglru_step",
        grid=(1,),
        in_specs=[full((n, D_RNN)), full((n, D_RNN)), full((CONV_W - 1, n, D_RNN)), full((n, D_RNN)),
                  full((CONV_W, D_RNN)), vec, full((D_RNN, D_RNN)), vec, full((D_RNN, D_RNN)), vec, vec],
        out_specs=[full((n, D_RNN)), full((n, D_RNN))],
        out_shape=[jax.ShapeDtypeStruct((n, D_RNN), BF16), jax.ShapeDtypeStruct((n, D_RNN), F32)],
        compiler_params=_cparams("arbitrary"),
    )(xr, gr, sc_t, h0, prm['conv_w'], prm['conv_b'], prm['w_rg'], prm['b_rg'], prm['w_ig'], prm['b_ig'], prm['lam'])


def _kth_largest_threshold(stages, shape, n_sel):
    carry = (jnp.full(shape, INT_MIN, jnp.int32), jnp.zeros(shape, F32))
    top = 31
    for count_ge, nbits in stages:

        def bit_step(t, carry, count_ge=count_ge, top=top):
            r, cnt_r = carry
            cand = r + jnp.left_shift(jnp.int32(1), top - t)
            cnt = count_ge(_ordered_to_f32(cand))
            keep = cnt >= float(n_sel)
            return jnp.where(keep, cand, r), jnp.where(keep, cnt, cnt_r)

        carry = lax.fori_loop(0, nbits, bit_step, carry)
        top -= nbits
    r, cnt_r = carry
    return jnp.where(r != INT_MIN, _ordered_to_f32(r), F32_LOWEST), cnt_r


def _kth_largest_via_bf16(count_ge_rounded, count_ge, shape, n_sel):
    k = float(n_sel)

    def grid_f32(p):
        return lax.bitcast_convert_type(jnp.left_shift(jnp.where(p >= 0, p, p ^ jnp.int32(0x7FFF)), 16), F32)

    def coarse_step(t, p):
        cand = p + jnp.left_shift(jnp.int32(1), 15 - t)
        return jnp.where(count_ge_rounded(grid_f32(cand)) >= k, cand, p)

    p_min = -(1 << 15)
    p = lax.fori_loop(0, 16, coarse_step, jnp.full(shape, p_min, jnp.int32))
    found = p != p_min
    centre = jnp.where(p >= 0, jnp.left_shift(p, 16), jnp.left_shift(p, 16) | jnp.int32(0xFFFF))
    lo = jnp.where(found, centre - ((1 << 15) + 1), 0)
    hi = jnp.where(found, centre + ((1 << 16) + 1), 1)

    def unsettled(lo, hi, cnt_lo):
        return jnp.max(jnp.where((hi - lo > 1) & (cnt_lo != k), 1.0, 0.0)) > 0.0

    def cond(state):
        t, _, _, _, go = state
        return jnp.logical_and(t < 18, go)

    def body(state):
        t, lo, hi, cnt_lo, _ = state
        active = hi - lo > 1
        mid = lo + jnp.right_shift(hi - lo, 1)
        cnt = count_ge(_ordered_to_f32(mid))
        up = active & (cnt >= k)
        down = active & (cnt < k)
        lo, cnt_lo, hi = jnp.where(up, mid, lo), jnp.where(up, cnt, cnt_lo), jnp.where(down, mid, hi)
        return t + 1, lo, hi, cnt_lo, unsettled(lo, hi, cnt_lo)

    cnt0 = jnp.full(shape, -1.0, F32)
    _, lo, _, cnt_lo, _ = lax.while_loop(cond, body, (jnp.int32(0), lo, hi, cnt0, unsettled(lo, hi, cnt0)))
    return jnp.where(found, _ordered_to_f32(lo), F32_LOWEST), cnt_lo


def _tie_cut(count_eq_lt, need, nbits, shape):
    def step(t, x):
        cand = x + jnp.left_shift(jnp.int32(1), nbits - 1 - t)
        return jnp.where(count_eq_lt(cand) < need, cand, x)

    return lax.fori_loop(0, nbits, step, jnp.zeros(shape, jnp.int32))


def _keep_f32(s, thr, kpos, cut):
    return jnp.where(s == thr, jnp.where(kpos <= cut, 1.0, 0.0), jnp.where(s > thr, 1.0, 0.0))


def _attn_body(qi_ref, q_ref, wt_ref, kid_ref, kb_ref, vt_ref, qg_ref, kg_ref, ob_ref,
               sc_ref, sb_ref, qis_ref, qs_ref, st_ref, pt_ref, m_ref, l_ref, acc_ref, *, n_sel, s_len):
    tq = q_ref.shape[1]
    kc = vt_ref.shape[3]
    npair = N_HEADS // 2
    nlb = tq // LANES
    i = pl.program_id(1)
    nch = (i + 1) * (tq // kc)
    lane = lax.broadcasted_iota(jnp.int32, (1, LANES), 1)
    low_head = lane < HEAD_DIM

    for j in range(npair):
        for src, dst in ((qi_ref, qis_ref), (q_ref, qs_ref)):
            blk = src[0, :, LANES * j:LANES * (j + 1)]
            zero = jnp.zeros_like(blk)
            dst[j, 0:tq, :] = jnp.where(low_head, blk, zero)
            dst[j, tq:2 * tq, :] = jnp.where(low_head, zero, blk)

    qpos = i * tq + lax.broadcasted_iota(jnp.int32, (1, tq), 1)
    kiota = lax.broadcasted_iota(jnp.int32, (kc, 1), 0)
    wt = wt_ref[0]

    npair_chunks = (nch + 1) // 2

    def key_rows(c):
        return pl.ds(pl.multiple_of(jnp.minimum(c, s_len // kc - 1) * kc, kc), kc)

    def index_matmuls(c, slot):
        kic = kid_ref[0, key_rows(c), :]
        for j in range(npair):
            st_ref[slot, j] = lax.dot_general(kic, qis_ref[j], NT_DIMS, preferred_element_type=F32)

    def index_scores(c, slot):
        kpos = c * kc + kiota
        for rb in range(kc // LANES):
            rows = slice(rb * LANES, (rb + 1) * LANES)
            for lb in range(nlb):
                cols = slice(lb * LANES, (lb + 1) * LANES)
                acc = jnp.zeros((LANES, LANES), F32)
                for h in range(IDX_HEADS):
                    j, half = divmod(h, 2)
                    sh = st_ref[slot, j, rows, half * tq + lb * LANES:half * tq + (lb + 1) * LANES]
                    acc = acc + wt[h:h + 1, cols] * jnp.maximum(sh, 0.0)
                val = jnp.where(kpos[rows] <= qpos[:, cols], acc * IDX_SCALE, -jnp.inf)
                sc_ref[c, rows, cols] = val
                sb_ref[c, rows, cols] = val.astype(BF16)

    def index_pair(c2, carry):
        c = 2 * c2
        index_matmuls(c + 1, 1)
        index_scores(c, 0)
        index_matmuls(c + 2, 0)
        index_scores(c + 1, 1)
        return carry

    index_matmuls(0, 0)
    lax.fori_loop(0, npair_chunks, index_pair, 0)

    def count_rows(pred):
        def chunk2(c2, acc):
            for u in range(2):
                c = 2 * c2 + u
                hit = pred(sc_ref[c], c * kc + kiota)
                acc = acc + jnp.sum(hit.reshape(kc // SUBLANES, SUBLANES, tq), axis=0)
            return acc

        acc = lax.fori_loop(0, npair_chunks, chunk2, jnp.zeros((SUBLANES, tq), F32))
        return jnp.sum(acc, axis=0, keepdims=True)

    bf16_rows = 2 * SUBLANES
    one_b = jnp.ones((bf16_rows, tq), BF16)
    zero_b = jnp.zeros((bf16_rows, tq), BF16)

    def count_ge_rounded(cand):
        cb = jnp.broadcast_to(cand, (bf16_rows, tq)).astype(BF16)

        def chunk2(c2, acc):
            for u in range(2):
                sb = sb_ref[2 * c2 + u]
                hits = [jnp.where(sb[bf16_rows * g:bf16_rows * (g + 1)] >= cb, one_b, zero_b)
                        for g in range(kc // bf16_rows)]
                while len(hits) > 1:
                    hits = [a + b for a, b in zip(hits[0::2], hits[1::2])]
                acc = acc + hits[0]
            return acc

        acc = lax.fori_loop(0, npair_chunks, chunk2, zero_b)
        return jnp.sum(acc.astype(F32), axis=0, keepdims=True)

    def count_ge(cand):
        return count_rows(lambda s, kpos: jnp.where(s >= cand, 1.0, 0.0))

    assert (kc // bf16_rows) * s_len // kc <= 256
    thr, cnt = _kth_largest_via_bf16(count_ge_rounded, count_ge, (1, tq), n_sel)
    has_ties = jnp.max(cnt) > float(n_sel)

    bound = 1.02 * HEAD_DIM * Q_SCALE * jnp.max(jnp.abs(qg_ref[...])) * jnp.max(jnp.abs(kg_ref[...]))
    fast = 2.0 * bound <= 120.0
    sel_bias = jnp.where(fast, -bound, 0.0)

    @pl.when(jnp.logical_not(has_ties))
    def _():
        def to_bias(c, carry):
            sc_ref[c] = jnp.where(sc_ref[c] >= thr, sel_bias, NEG)
            return carry

        lax.fori_loop(0, nch, to_bias, 0)

    @pl.when(has_ties)
    def _():
        need = float(n_sel) - count_rows(lambda s, kpos: jnp.where(s > thr, 1.0, 0.0))
        cut = _tie_cut(
            lambda cand: count_rows(lambda s, kpos: jnp.where(s == thr, jnp.where(kpos < cand, 1.0, 0.0), 0.0)),
            need, (s_len - 1).bit_length() + 1, (1, tq))

        def to_bias(c, carry):
            keep = _keep_f32(sc_ref[c], thr, c * kc + kiota, cut)
            sc_ref[c] = jnp.where(keep > 0.5, sel_bias, NEG)
            return carry

        lax.fori_loop(0, nch, to_bias, 0)

    m_ref[...] = jnp.full(m_ref.shape, NEG, F32)
    l_ref[...] = jnp.zeros(l_ref.shape, F32)
    acc_ref[...] = jnp.zeros(acc_ref.shape, F32)

    def logit_matmuls(c, slot):
        for j in range(npair):
            kj = kb_ref[0, key_rows(c), LANES * j:LANES * (j + 1)]
            st_ref[slot, j] = lax.dot_general(kj, qs_ref[j], NT_DIMS, preferred_element_type=F32)

    def weights_and_values(c, slot):
        for h in range(N_HEADS):
            j, half = divmod(h, 2)
            for lb in range(nlb):
                cols = slice(lb * LANES, (lb + 1) * LANES)
                p = jnp.exp2(st_ref[slot, j, :, half * tq + lb * LANES:half * tq + (lb + 1) * LANES]
                             + sc_ref[c, :, cols])
                l_ref[h:h + 1, cols] = l_ref[h:h + 1, cols] + jnp.sum(p, axis=0, keepdims=True)
                pt_ref[h, :, cols] = p.astype(BF16)
        for h in range(N_HEADS):
            hrows = slice(HEAD_DIM * h, HEAD_DIM * (h + 1))
            acc_ref[hrows, :] = acc_ref[hrows, :] + jnp.dot(vt_ref[0, c, hrows, :], pt_ref[h],
                                                            preferred_element_type=F32)

    def attend_shifted(c, carry):
        logit_matmuls(c, 0)
        weights_and_values(c, 0)
        return carry

    def attend_online(c, carry):
        logit_matmuls(c, 0)
        for h in range(N_HEADS):
            j, half = divmod(h, 2)
            alphas = []
            for lb in range(nlb):
                cols = slice(lb * LANES, (lb + 1) * LANES)
                s = st_ref[0, j, :, half * tq + lb * LANES:half * tq + (lb + 1) * LANES] + sc_ref[c, :, cols]
                m_old = m_ref[h:h + 1, cols]
                m_new = jnp.maximum(m_old, jnp.max(s, axis=0, keepdims=True))
                alpha = jnp.exp2(m_old - m_new)
                p = jnp.exp2(s - m_new)
                l_ref[h:h + 1, cols] = alpha * l_ref[h:h + 1, cols] + jnp.sum(p, axis=0, keepdims=True)
                m_ref[h:h + 1, cols] = m_new
                pt_ref[h, :, cols] = p.astype(BF16)
                alphas.append(alpha)
            hrows = slice(HEAD_DIM * h, HEAD_DIM * (h + 1))
            pv = jnp.dot(vt_ref[0, c, hrows, :], pt_ref[h], preferred_element_type=F32)
            acc_ref[hrows, :] = jnp.concatenate(alphas, axis=1) * acc_ref[hrows, :] + pv
        return carry

    @pl.when(fast)
    def _():
        lax.fori_loop(0, nch, attend_shifted, 0)

    @pl.when(jnp.logical_not(fast))
    def _():
        lax.fori_loop(0, nch, attend_online, 0)

    linv = 1.0 / l_ref[...]
    for h in range(N_HEADS):
        hrows = slice(HEAD_DIM * h, HEAD_DIM * (h + 1))
        acc_ref[hrows, :] = acc_ref[hrows, :] * linv[h:h + 1]
    ob_ref[0] = acc_ref[...].T.astype(BF16)


def _attn_prompt(qi, q, wt, kid, kb, vt, qg, kg, n_sel, tq):
    b, s, _ = q.shape
    nq = s // tq
    kc = vt.shape[3]
    nkc = s // kc
    blk = lambda i, j: (i, j, 0)
    per_b = lambda i, j: (i, 0, 0)
    gain = pl.BlockSpec((1, D_ATT), lambda i, j: (0, 0))
    return pl.pallas_call(
        functools.partial(_attn_body, n_sel=n_sel, s_len=s),
        name="attn_prompt",
        grid=(b, nq),
        in_specs=[pl.BlockSpec((1, tq, D_ATT), blk), pl.BlockSpec((1, tq, D_ATT), blk),
                  pl.BlockSpec((1, IDX_HEADS, tq), lambda i, j: (i, 0, j)), pl.BlockSpec((1, s, LANES), per_b),
                  pl.BlockSpec((1, s, D_ATT), per_b), pl.BlockSpec((1, nkc, D_ATT, kc), lambda i, j: (i, 0, 0, 0)),
                  gain, gain],
        out_specs=pl.BlockSpec((1, tq, D_ATT), blk),
        out_shape=jax.ShapeDtypeStruct((b, s, D_ATT), BF16),
        scratch_shapes=[pltpu.VMEM((nkc + nkc % 2, kc, tq), F32),
                        pltpu.VMEM((nkc + nkc % 2, kc, tq), BF16),
                        pltpu.VMEM((N_HEADS // 2, 2 * tq, LANES), BF16),
                        pltpu.VMEM((N_HEADS // 2, 2 * tq, LANES), BF16),
                        pltpu.VMEM((2, N_HEADS // 2, kc, 2 * tq), F32),
                        pltpu.VMEM((N_HEADS, kc, tq), BF16),
                        pltpu.VMEM((N_HEADS, tq), F32),
                        pltpu.VMEM((N_HEADS, tq), F32),
                        pltpu.VMEM((D_ATT, tq), F32)],
        compiler_params=_cparams("parallel", "arbitrary"),
    )(qi, q, wt, kid, kb, vt, qg, kg)


def _sample_index_body(pt_ref, qh_ref, wcol_ref, *rest):
    del pt_ref
    *page_refs, out_ref = rest
    kt = jnp.concatenate([r[0] for r in page_refs], axis=1).astype(BF16)
    s = jnp.dot(qh_ref[0], kt, preferred_element_type=F32)
    t = jnp.maximum(s, 0.0) * wcol_ref[0]
    out_ref[0] = jnp.sum(t, axis=0, keepdims=True) * IDX_SCALE


def _sample_index(page_table, qh, wcol, cki_t, pp):
    bd, npg = page_table.shape
    per_b = lambda b, c, pt: (b, 0, 0)
    page_spec = lambda j: pl.BlockSpec((1, IDX_DIM, PAGE_SIZE), lambda b, c, pt: (pt[b, c * pp + j], 0, 0))
    grid_spec = pltpu.PrefetchScalarGridSpec(
        num_scalar_prefetch=1,
        grid=(bd, npg // pp),
        in_specs=[pl.BlockSpec((1, IDX_HEADS, IDX_DIM), per_b), pl.BlockSpec((1, IDX_HEADS, 1), per_b)]
                 + [page_spec(j) for j in range(pp)],
        out_specs=pl.BlockSpec((1, 1, pp * PAGE_SIZE), lambda b, c, pt: (b, 0, c)),
    )
    return pl.pallas_call(
        _sample_index_body,
        name="sample_index",
        grid_spec=grid_spec,
        out_shape=jax.ShapeDtypeStruct((bd, 1, npg * PAGE_SIZE), F32),
        compiler_params=_cparams("parallel", "arbitrary"),
    )(page_table, qh, wcol, *([cki_t] * pp))


def _sample_select_body(sc_ref, qi_ref, kid_ref, zkw_ref, bdh_ref, bias_ref, biasn_ref, *, n_sel, chunk):
    rows, past = sc_ref.shape
    lane = lax.broadcasted_iota(jnp.int32, (1, LANES), 1)
    kid = kid_ref[...].astype(F32)
    prod = qi_ref[...].astype(F32) * jnp.concatenate([kid] * (IDX_HEADS * IDX_DIM // LANES), axis=1)
    sh = jnp.zeros((rows, LANES), F32)
    for part in _split3(prod):
        sh = sh + jnp.dot(part, bdh_ref[...], preferred_element_type=F32)
    wi = pltpu.roll(zkw_ref[...], LANES - IDX_DIM, 1)
    new = jnp.sum(jnp.where(lane < IDX_HEADS, jnp.maximum(sh, 0.0) * wi, 0.0), axis=-1, keepdims=True) * IDX_SCALE

    nchunk = past // chunk
    ciota = lax.broadcasted_iota(jnp.int32, (1, chunk), 1)

    def count_rows(pred):
        def cnt_chunk(c, acc):
            off = pl.multiple_of(c * chunk, chunk)
            hit = pred(sc_ref[:, pl.ds(off, chunk)], off + ciota)
            for g in range(chunk // LANES):
                acc = acc + hit[:, LANES * g:LANES * (g + 1)]
            return acc

        acc = lax.fori_loop(0, nchunk, cnt_chunk, jnp.zeros((rows, LANES), F32))
        return jnp.sum(acc, axis=-1, keepdims=True)

    thr, _ = _kth_largest_threshold(
        [(lambda cand: count_rows(lambda s, kpos: jnp.where(s >= cand, 1.0, 0.0)) + jnp.where(new >= cand, 1.0, 0.0),
          32)], (rows, 1), n_sel)
    need = float(n_sel) - (count_rows(lambda s, kpos: jnp.where(s > thr, 1.0, 0.0)) + jnp.where(new > thr, 1.0, 0.0))
    cut = _tie_cut(
        lambda cand: count_rows(lambda s, kpos: jnp.where(s == thr, jnp.where(kpos < cand, 1.0, 0.0), 0.0))
        + jnp.where(new == thr, jnp.where(past < cand, 1.0, 0.0), 0.0),
        need, past.bit_length() + 1, (rows, 1))

    def emit(c, carry):
        off = pl.multiple_of(c * chunk, chunk)
        keep = _keep_f32(sc_ref[:, pl.ds(off, chunk)], thr, off + ciota, cut)
        bias_ref[:, pl.ds(off, chunk)] = jnp.where(keep > 0.5, 0.0, NEG)
        return carry

    lax.fori_loop(0, nchunk, emit, 0)
    keep_new = _keep_f32(new, thr, jnp.int32(past), cut)
    biasn_ref[...] = jnp.broadcast_to(jnp.where(keep_new > 0.5, 0.0, NEG), (rows, LANES))


def _sample_select(sc_nat, qi, kid, zkw, bdh, n_sel):
    bd, past = sc_nat.shape
    rows = min(bd, 128)
    chunk = min(past, 512)
    rb = lambda i: (i, 0)
    return pl.pallas_call(
        functools.partial(_sample_select_body, n_sel=n_sel, chunk=chunk),
        name="sample_select",
        grid=(bd // rows,),
        in_specs=[pl.BlockSpec((rows, past), rb), pl.BlockSpec((rows, IDX_HEADS * IDX_DIM), rb),
                  pl.BlockSpec((rows, LANES), rb), pl.BlockSpec((rows, LANES), rb),
                  pl.BlockSpec((IDX_HEADS * IDX_DIM, LANES), lambda i: (0, 0))],
        out_specs=[pl.BlockSpec((rows, past), rb), pl.BlockSpec((rows, LANES), rb)],
        out_shape=[jax.ShapeDtypeStruct((bd, past), F32), jax.ShapeDtypeStruct((bd, LANES), F32)],
        compiler_params=_cparams("parallel"),
    )(sc_nat, qi, kid, zkw, bdh)


def _sample_attend_body(pt_ref, q_ref, bias_ref, biasn_ref, kn_ref, vn_ref, *rest, pp):
    del pt_ref
    k_refs = rest[:pp]
    v_refs = rest[pp:2 * pp]
    ob_ref, m_ref, l_ref, acc_ref = rest[2 * pp:]
    c = pl.program_id(1)

    @pl.when(c == 0)
    def _():
        m_ref[...] = jnp.full(m_ref.shape, NEG, F32)
        l_ref[...] = jnp.zeros(l_ref.shape, F32)
        acc_ref[...] = jnp.zeros(acc_ref.shape, F32)

    sub = lax.broadcasted_iota(jnp.int32, (N_HEADS, D_ATT), 0)
    lane = lax.broadcasted_iota(jnp.int32, (N_HEADS, D_ATT), 1)
    own = lane // HEAD_DIM == sub
    qrow = jnp.broadcast_to(q_ref[0].astype(F32), (N_HEADS, D_ATT))
    qbd = jnp.where(own, qrow, 0.0)

    def pages_t(refs):
        return jnp.concatenate([r[0].reshape(D_ATT, PAGE_SIZE) for r in refs], axis=1).astype(BF16)

    s = jnp.dot(qbd.astype(BF16), pages_t(k_refs), preferred_element_type=F32) + bias_ref[0]
    m_old = m_ref[...]
    m_new = jnp.maximum(m_old, jnp.max(s, axis=-1, keepdims=True))
    alpha = jnp.exp2(m_old - m_new)
    p = jnp.exp2(s - m_new)
    l_ref[...] = alpha * l_ref[...] + jnp.sum(p, axis=-1, keepdims=True)
    m_ref[...] = m_new
    pv = lax.dot_general(p.astype(BF16), pages_t(v_refs), NT_DIMS, preferred_element_type=F32)
    acc_ref[...] = alpha * acc_ref[...] + pv

    @pl.when(c == pl.num_programs(1) - 1)
    def _():
        s_new = jnp.sum(qbd * kn_ref[0].astype(F32), axis=-1, keepdims=True) + biasn_ref[0][:, 0:1]
        m_old = m_ref[...]
        m_new = jnp.maximum(m_old, s_new)
        alpha = jnp.exp2(m_old - m_new)
        p_new = jnp.exp2(s_new - m_new)
        l = alpha * l_ref[...] + p_new
        acc = alpha * acc_ref[...] + p_new.astype(BF16).astype(F32) * vn_ref[0].astype(F32)
        ob_ref[0] = jnp.sum(jnp.where(own, acc / l, 0.0), axis=0, keepdims=True).astype(BF16)


def _sample_attend(page_table, q, bias, biasn, kn, vn, ck_t, cv_t, pp):
    bd, npg = page_table.shape
    per_b = lambda b, c, pt: (b, 0, 0)
    row_spec = pl.BlockSpec((1, 1, D_ATT), per_b)
    page_spec = lambda j: pl.BlockSpec((1, N_HEADS, HEAD_DIM, PAGE_SIZE),
                                       lambda b, c, pt: (pt[b, c * pp + j], 0, 0, 0))
    grid_spec = pltpu.PrefetchScalarGridSpec(
        num_scalar_prefetch=1,
        grid=(bd, npg // pp),
        in_specs=[row_spec, pl.BlockSpec((1, 1, pp * PAGE_SIZE), lambda b, c, pt: (b, 0, c)),
                  pl.BlockSpec((1, 1, LANES), per_b), row_spec, row_spec] + [page_spec(j) for j in range(pp)] * 2,
        out_specs=row_spec,
        scratch_shapes=[pltpu.VMEM((N_HEADS, 1), F32), pltpu.VMEM((N_HEADS, 1), F32),
                        pltpu.VMEM((N_HEADS, D_ATT), F32)],
    )
    return pl.pallas_call(
        functools.partial(_sample_attend_body, pp=pp),
        name="sample_attend",
        grid_spec=grid_spec,
        out_shape=jax.ShapeDtypeStruct((bd, 1, D_ATT), BF16),
        compiler_params=_cparams("parallel", "arbitrary"),
    )(page_table, q, bias, biasn, kn, vn, *([ck_t] * pp), *([cv_t] * pp))


def _merge_body(x_ref, oa_ref, ob_ref, ga_ref, gb_ref, wa_ref, wb_ref, wo_ref, g2_ref, wfg_ref, wfu_ref, wfd_ref,
                y_ref, *, ff_chunk):
    ma = jnp.dot(oa_ref[...], wa_ref[...], preferred_element_type=F32)
    mb = jnp.dot(ob_ref[...], wb_ref[...], preferred_element_type=F32)
    m = _sigmoid(ga_ref[...]) * ma + _sigmoid(gb_ref[...]) * mb
    x1 = x_ref[...] + jnp.dot(m.astype(BF16), wo_ref[...], preferred_element_type=F32)
    ms = jnp.mean(x1 * x1, axis=-1, keepdims=True)
    h = (x1 * lax.rsqrt(ms + EPS) * g2_ref[...]).astype(BF16)
    d_ff = wfg_ref.shape[1]
    y = x1
    for lo in range(0, d_ff, ff_chunk):
        g = jnp.dot(h, wfg_ref[:, lo:lo + ff_chunk], preferred_element_type=F32)
        u = jnp.dot(h, wfu_ref[:, lo:lo + ff_chunk], preferred_element_type=F32)
        act = (g * _sigmoid(g) * u).astype(BF16)
        y = y + jnp.dot(act, wfd_ref[lo:lo + ff_chunk, :], preferred_element_type=F32)
    y_ref[...] = y


def _merge(x2d, oa, ob, ga, gb, prm, tm):
    n = x2d.shape[0]
    row = lambda i: (i, 0)
    d_ff = prm['w_fg'].shape[1]
    ff_chunk = 256 if d_ff % 256 == 0 else d_ff
    return pl.pallas_call(
        functools.partial(_merge_body, ff_chunk=ff_chunk),
        name="merge_ffn",
        grid=(n // tm,),
        in_specs=[pl.BlockSpec((tm, D_MODEL), row), pl.BlockSpec((tm, D_RNN), row), pl.BlockSpec((tm, D_ATT), row),
                  pl.BlockSpec((tm, D_MODEL), row), pl.BlockSpec((tm, D_MODEL), row),
                  _const_spec(prm['w_a'].shape), _const_spec(prm['w_b'].shape), _const_spec(prm['w_o'].shape),
                  _const_spec((1, D_MODEL)), _const_spec(prm['w_fg'].shape), _const_spec(prm['w_fu'].shape),
                  _const_spec(prm['w_fd'].shape)],
        out_specs=pl.BlockSpec((tm, D_MODEL), row),
        out_shape=jax.ShapeDtypeStruct((n, D_MODEL), F32),
        compiler_params=_cparams("parallel"),
    )(x2d, oa, ob, ga, gb, prm['w_a'], prm['w_b'], prm['w_o'], prm['g2'], prm['w_fg'], prm['w_fu'], prm['w_fd'])


def _rope_tables(pos):
    half = HEAD_DIM // 2
    inv = 1.0 / (ROPE_THETA ** (jnp.arange(half, dtype=F32) / half))
    ang = pos.astype(F32)[:, None] * inv[None, :]
    cos, sin = jnp.cos(ang), jnp.sin(ang)
    cos_h = jnp.concatenate([cos, cos], axis=-1)
    sin_h = jnp.concatenate([-sin, sin], axis=-1)
    return jnp.tile(cos_h, (1, LANES // HEAD_DIM)), jnp.tile(sin_h, (1, LANES // HEAD_DIM))


def _block_diag(w):
    nb, bw, _ = w.shape
    eye = jnp.eye(nb, dtype=w.dtype)
    return (eye[:, None, :, None] * w[:, :, None, :]).reshape(nb * bw, nb * bw)


def _layer_params(l, norm1_g, w_in, conv_w, conv_b, w_rg, b_rg, w_ig, b_ig, lru_lambda, q_norm_g, k_norm_g,
                  k_idx_norm_g, w_branch_a, w_branch_b, w_out, norm2_g, w_ffn_gate, w_ffn_up, w_ffn_down):
    w = w_in[l]
    o_ki = 2 * D_RNN + 3 * D_ATT + IDX_HEADS * IDX_DIM
    o_g = o_ki + IDX_DIM + IDX_HEADS
    w_kw = jnp.pad(w[:, o_ki:o_g], ((0, 0), (0, LANES - IDX_DIM - IDX_HEADS)))
    head_of = np.arange(D_ATT) // HEAD_DIM
    bd_head = jnp.asarray((head_of[:, None] == head_of[None, :]) / HEAD_DIM, BF16)
    bd_sum = jnp.asarray(head_of[:, None] == np.arange(LANES)[None, :], BF16)
    tile_h = lambda g: jnp.tile(g[l], N_HEADS)[None, :]
    return dict(
        g1=norm1_g[l][None, :], w_main=w[:, :o_ki].astype(BF16), w_kw=w_kw.astype(BF16),
        w_gate=w[:, o_g:].astype(BF16), qg=tile_h(q_norm_g), kg=tile_h(k_norm_g),
        kig=jnp.pad(k_idx_norm_g[l], (0, LANES - IDX_DIM))[None, :], bd_head=bd_head, bd_sum=bd_sum,
        conv_w=conv_w[l], conv_b=conv_b[l][None, :], w_rg=_block_diag(w_rg[l]).astype(BF16), b_rg=b_rg[l][None, :],
        w_ig=_block_diag(w_ig[l]).astype(BF16), b_ig=b_ig[l][None, :], lam=lru_lambda[l][None, :],
        w_a=w_branch_a[l].astype(BF16), w_b=w_branch_b[l].astype(BF16), w_o=w_out[l].astype(BF16),
        g2=norm2_g[l][None, :], w_fg=w_ffn_gate[l].astype(BF16), w_fu=w_ffn_up[l].astype(BF16),
        w_fd=w_ffn_down[l].astype(BF16))


def _pick_tile(n, pref):
    t = min(n, pref)
    while n % t:
        t //= 2
    return t


def _prompt_layer(x, prm):
    b, s, _ = x.shape
    n_sel = min(TOPK_MAX, s // 4)
    tm = _pick_tile(s, 256)
    tq = _pick_tile(s, 512)
    cos, sin = _rope_tables(jnp.arange(s))
    xr, gr, q, kt, kb, vt, vb, vtb, qi, kit, kid, zkw, ga, gb = _proj(x.reshape(b * s, D_MODEL), cos, sin, prm, tm, s)
    r3 = lambda a: a.reshape(b, s, a.shape[-1])
    xr3 = r3(xr)
    oa, h_last = _rglru_seq(xr3, r3(gr), jnp.zeros((b, SUBLANES, D_RNN), F32), jnp.zeros((b, 1, D_RNN), F32), prm,
                            _pick_tile(s, 256))
    wt = r3(zkw)[:, :, IDX_DIM:IDX_DIM + IDX_HEADS].transpose(0, 2, 1)
    ob = _attn_prompt(r3(qi), r3(q), wt, r3(kid), r3(kb), vtb, prm['qg'], prm['kg'], n_sel, tq)
    y = _merge(x.reshape(b * s, D_MODEL), oa.reshape(b * s, D_RNN), ob.reshape(b * s, D_ATT), ga, gb, prm,
               _pick_tile(s, 512))
    xpad = jnp.concatenate([jnp.zeros((b, CONV_W - 1, D_RNN), F32), xr3], axis=1)
    heads_last = lambda a: a.reshape(b, N_HEADS, HEAD_DIM, s).transpose(0, 3, 1, 2)
    return (y.reshape(b, s, D_MODEL), heads_last(kt), heads_last(vt), kit.transpose(0, 2, 1),
            xpad[:, -(CONV_W - 1):], h_last.reshape(b, D_RNN))


def _sample_layer(x, cache_k, cache_v, cache_k_idx, state_conv, state_h, page_table, prm):
    bd, t, _ = x.shape
    assert t == 1
    npg = page_table.shape[1]
    past = npg * PAGE_SIZE
    n_sel = min(TOPK_MAX, (past + t) // 4)
    cos, sin = _rope_tables(jnp.full((bd,), past))
    xr, gr, q, kt, kb, vt, vb, _, qi, kit, kid, zkw, ga, gb = _proj(x.reshape(bd, D_MODEL), cos, sin, prm, bd, bd)
    oa, h_new = _rglru_step(xr, gr, state_conv.transpose(1, 0, 2), state_h, prm)

    qh = qi.reshape(bd, IDX_HEADS, IDX_DIM)
    wcol = zkw[:, IDX_DIM:IDX_DIM + IDX_HEADS, None]
    sc = _sample_index(page_table, qh, wcol, cache_k_idx.transpose(0, 2, 1), _pick_tile(npg, 32))
    bias, biasn = _sample_select(sc.reshape(bd, past), qi, kid, zkw, prm['bd_sum'], n_sel)
    ob = _sample_attend(page_table, q[:, None, :], bias[:, None, :], biasn[:, None, :], kb[:, None, :],
                        vb[:, None, :], cache_k.transpose(0, 2, 3, 1), cache_v.transpose(0, 2, 3, 1),
                        _pick_tile(npg, 16))
    y = _merge(x.reshape(bd, D_MODEL), oa, ob.reshape(bd, D_ATT), ga, gb, prm, bd)
    conv_new = jnp.concatenate([state_conv, xr[:, None, :]], axis=1)[:, -(CONV_W - 1):]
    heads_last = lambda a: a.reshape(N_HEADS, HEAD_DIM, bd).transpose(2, 0, 1)[:, None]
    return (y.reshape(bd, 1, D_MODEL), heads_last(kt), heads_last(vt), kit[0].T[:, None, :], conv_new, h_new)


def kernel(x_prompt, x_sample, cache_k, cache_v, cache_k_idx, state_conv, state_h, page_table, norm1_g, w_in, conv_w, conv_b, w_rg, b_rg, w_ig, b_ig, lru_lambda, q_norm_g, k_norm_g, k_idx_norm_g, w_branch_a, w_branch_b, w_out, norm2_g, w_ffn_gate, w_ffn_up, w_ffn_down):
    depth = w_in.shape[0]
    yp, ys = x_prompt, x_sample
    outs_p, outs_s = [], []
    for l in range(depth):
        prm = _layer_params(l, norm1_g, w_in, conv_w, conv_b, w_rg, b_rg, w_ig, b_ig, lru_lambda, q_norm_g,
                            k_norm_g, k_idx_norm_g, w_branch_a, w_branch_b, w_out, norm2_g, w_ffn_gate, w_ffn_up,
                            w_ffn_down)
        yp, *rest_p = _prompt_layer(yp, prm)
        ys, *rest_s = _sample_layer(ys, cache_k[l], cache_v[l], cache_k_idx[l], state_conv[l], state_h[l],
                                    page_table, prm)
        outs_p.append(rest_p)
        outs_s.append(rest_s)
    stack = lambda outs, i: jnp.stack([o[i] for o in outs])
    return (yp, ys, *[stack(outs_p, i) for i in range(5)], *[stack(outs_s, i) for i in range(5)])
```
